```python
import math
import jax
import jax.numpy as jnp
from jax import lax
import numpy as np

D_MODEL = 1024
BATCH = 8
SEQ = 4096
DEPTH = 2

D_MIX = D_MODEL
ATTN_HEADS = 8
HEAD_DIM = 64
D_ATTN = ATTN_HEADS * HEAD_DIM
D_SSM = D_MIX // 4
SSM_GROUP = 16
SSM_GROUPS = D_SSM // SSM_GROUP
SSM_STATE = 64
D_CONV = D_MIX - D_ATTN - D_SSM
CONV_WIDTH = 31
ROT_DIM = HEAD_DIM // 4
ROPE_THETA = 500000.0
MOBA_BLOCK = 256
MOBA_TOPK = 3
MOBA_QCHUNK = 32
D_FF = ((8 * D_MODEL + 3 * 256 - 1) // (3 * 256)) * 256
D_IN = 3 * D_ATTN + D_SSM + 2 * D_CONV
SPLITS = (D_ATTN, 2 * D_ATTN, 3 * D_ATTN, 3 * D_ATTN + D_SSM, 3 * D_ATTN + D_SSM + D_CONV)
RMS_EPS = 1e-6
LN_EPS = 1e-5

kernel_name = "hymba_moba_s5_conformer_hybrid"


def rms_norm(x, g):
    xf = x.astype(jnp.float32)
    y = xf * lax.rsqrt(jnp.mean(xf * xf, axis=-1, keepdims=True) + RMS_EPS)
    return (y * g.astype(jnp.float32)).astype(x.dtype)


def layer_norm(x, g, b):
    xf = x.astype(jnp.float32)
    mu = jnp.mean(xf, axis=-1, keepdims=True)
    xc = xf - mu
    y = xc * lax.rsqrt(jnp.mean(xc * xc, axis=-1, keepdims=True) + LN_EPS)
    return (y * g.astype(jnp.float32) + b.astype(jnp.float32)).astype(x.dtype)


def rope_tables(seq):
    inv = jnp.power(ROPE_THETA, -jnp.arange(0, ROT_DIM, 2, dtype=jnp.float32) / ROT_DIM)
    ang = jnp.arange(seq, dtype=jnp.float32)[:, None] * inv[None, :]
    return jnp.cos(ang), jnp.sin(ang)


def apply_partial_rope(t, cos, sin):
    half = ROT_DIM // 2
    c = cos.astype(t.dtype)
    s = sin.astype(t.dtype)
    t1 = t[..., :half]
    t2 = t[..., half:ROT_DIM]
    return jnp.concatenate([t1 * c - t2 * s, t2 * c + t1 * s, t[..., ROT_DIM:]], axis=-1)


def split_heads(t, bsz, seq):
    return t.reshape(bsz, seq, ATTN_HEADS, HEAD_DIM).transpose(0, 2, 1, 3)


def moba_attention(q, k, v):
    f32 = jnp.float32
    bsz, nh, seq, dh = q.shape
    nb = -(-seq // MOBA_BLOCK)
    pad = nb * MOBA_BLOCK - seq
    kp = jnp.pad(k, ((0, 0), (0, 0), (0, pad), (0, 0)))
    vp = jnp.pad(v, ((0, 0), (0, 0), (0, pad), (0, 0)))
    k_blocks = kp.reshape(bsz, nh, nb, MOBA_BLOCK, dh)
    v_blocks = vp.reshape(bsz, nh, nb, MOBA_BLOCK, dh)
    k_mean = jnp.mean(k_blocks.astype(f32), axis=3)
    topk = min(MOBA_TOPK, max(nb - 1, 1))
    scale = HEAD_DIM ** -0.5
    n_chunks = seq // MOBA_QCHUNK
    key_off = jnp.arange(MOBA_BLOCK)
    block_ids = jnp.arange(nb)
    gather = jax.vmap(jax.vmap(lambda blocks, idx: blocks[idx]))

    def chunk(ci):
        q0 = ci * MOBA_QCHUNK
        qc = lax.dynamic_slice_in_dim(q, q0, MOBA_QCHUNK, axis=2).astype(f32)
        qpos = q0 + jnp.arange(MOBA_QCHUNK)
        own = q0 // MOBA_BLOCK
        gate = jnp.einsum('bhqd,bhnd->bhqn', qc, k_mean)
        gate = jnp.where(block_ids < own, gate, -jnp.inf)
        _, sel = lax.top_k(gate, topk)
        sel_valid = jnp.arange(topk) < own
        k_sel = gather(k_blocks, sel).astype(f32)
        v_sel = gather(v_blocks, sel).astype(f32)
        k_own = lax.dynamic_index_in_dim(k_blocks, own, axis=2, keepdims=False).astype(f32)
        v_own = lax.dynamic_index_in_dim(v_blocks, own, axis=2, keepdims=False).astype(f32)
        s_sel = jnp.einsum('bhqd,bhqnkd->bhqnk', qc, k_sel) * scale
        s_sel = jnp.where(sel_valid[:, None], s_sel, -jnp.inf)
        s_sel = s_sel.reshape(bsz, nh, MOBA_QCHUNK, topk * MOBA_BLOCK)
        s_own = jnp.einsum('bhqd,bhkd->bhqk', qc, k_own) * scale
        causal = (own * MOBA_BLOCK + key_off)[None, :] <= qpos[:, None]
        s_own = jnp.where(causal, s_own, -jnp.inf)
        p = jax.nn.softmax(jnp.concatenate([s_own, s_sel], axis=-1), axis=-1)
        p_own = p[..., :MOBA_BLOCK]
        p_sel = p[..., MOBA_BLOCK:].reshape(bsz, nh, MOBA_QCHUNK, topk, MOBA_BLOCK)
        o = (jnp.einsum('bhqk,bhkd->bhqd', p_own, v_own)
             + jnp.einsum('bhqnk,bhqnkd->bhqd', p_sel, v_sel))
        return o.astype(q.dtype)

    out = lax.map(chunk, jnp.arange(n_chunks))
    return out.transpose(1, 2, 0, 3, 4).reshape(bsz, nh, seq, dh)


def s5_mixer(u, lam_re, lam_im, log_dt, b_re, b_im, c_re, c_im, d_skip, w_glu, b_glu):
    f32 = jnp.float32
    bsz, seq, _ = u.shape
    uf = u.astype(f32).reshape(bsz, seq, SSM_GROUPS, SSM_GROUP)
    dt = jnp.exp(log_dt.astype(f32))[:, None]
    lr = lam_re.astype(f32)
    li = lam_im.astype(f32)
    mag = jnp.exp(lr * dt)
    ab_re = mag * jnp.cos(li * dt)
    ab_im = mag * jnp.sin(li * dt)
    nr = ab_re - 1.0
    den = lr * lr + li * li
    f_re = (nr * lr + ab_im * li) / den
    f_im = (ab_im * lr - nr * li) / den
    br = b_re.astype(f32)
    bi = b_im.astype(f32)
    bb_re = f_re[..., None] * br - f_im[..., None] * bi
    bb_im = f_re[..., None] * bi + f_im[..., None] * br
    x_re = jnp.einsum('bsgc,gpc->bsgp', uf, bb_re)
    x_im = jnp.einsum('bsgc,gpc->bsgp', uf, bb_im)
    a_re = jnp.broadcast_to(ab_re, (1, seq, SSM_GROUPS, SSM_STATE))
    a_im = jnp.broadcast_to(ab_im, (1, seq, SSM_GROUPS, SSM_STATE))

    def combine(left, right):
        a1r, a1i, b1r, b1i = left
        a2r, a2i, b2r, b2i = right
        return (a2r * a1r - a2i * a1i,
                a2r * a1i + a2i * a1r,
                a2r * b1r - a2i * b1i + b2r,
                a2r * b1i + a2i * b1r + b2i)

    _, _, h_re, h_im = lax.associative_scan(combine, (a_re, a_im, x_re, x_im), axis=1)
    y = (jnp.einsum('bsgp,gcp->bsgc', h_re, c_re.astype(f32))
         - jnp.einsum('bsgp,gcp->bsgc', h_im, c_im.astype(f32))
         + d_skip.astype(f32) * uf)
    z = jax.nn.gelu(y.reshape(bsz, seq, D_SSM))
    out = z * jax.nn.sigmoid(z @ w_glu.astype(f32) + b_glu.astype(f32))
    return out.astype(u.dtype)


def conformer_conv(a, g, conv_w, conv_b, ln_g, ln_b, w_pw2, b_pw2):
    u = a * jax.nn.sigmoid(g)
    y = lax.conv_general_dilated(
        u, conv_w[:, None, :], window_strides=(1,),
        padding=((CONV_WIDTH - 1, 0),),
        dimension_numbers=('NWC', 'WIO', 'NWC'),
        feature_group_count=D_CONV) + conv_b
    y = jax.nn.silu(layer_norm(y, ln_g, ln_b))
    return y @ w_pw2 + b_pw2


def setup_inputs(seed: int = 0) -> dict:
    key = jax.random.key(seed)
    ks = jax.random.split(key, 32)
    f32 = jnp.float32
    L = DEPTH

    def nrm(k, shape, scale):
        return jax.random.normal(k, shape, f32) * scale

    def gain(k, shape):
        return 1.0 + 0.02 * jax.random.normal(k, shape, f32)

    n_idx = jnp.arange(SSM_STATE, dtype=f32)
    return {
        "x": nrm(ks[0], (BATCH, SEQ, D_MODEL), 1.0),
        "norm_mix_g": gain(ks[1], (L, D_MODEL)),
        "w_in": nrm(ks[2], (L, D_MODEL, D_IN), D_MODEL ** -0.5),
        "lambda_re": -0.5 + 0.01 * jax.random.normal(ks[3], (L, SSM_GROUPS, SSM_STATE), f32),
        "lambda_im": math.pi * n_idx + 0.01 * jax.random.normal(ks[4], (L, SSM_GROUPS, SSM_STATE), f32),
        "log_dt": jax.random.uniform(ks[5], (L, SSM_GROUPS), f32, math.log(1e-3), math.log(1e-1)),
        "b_re": nrm(ks[6], (L, SSM_GROUPS, SSM_STATE, SSM_GROUP), (2 * SSM_GROUP) ** -0.5),
        "b_im": nrm(ks[7], (L, SSM_GROUPS, SSM_STATE, SSM_GROUP), (2 * SSM_GROUP) ** -0.5),
        "c_re": nrm(ks[8], (L, SSM_GROUPS, SSM_GROUP, SSM_STATE), (2 * SSM_STATE) ** -0.5),
        "c_im": nrm(ks[9], (L, SSM_GROUPS, SSM_GROUP, SSM_STATE), (2 * SSM_STATE) ** -0.5),
        "d_skip": nrm(ks[10], (L, SSM_GROUPS, SSM_GROUP), 1.0),
        "w_ssm_glu": nrm(ks[11], (L, D_SSM, D_SSM), D_SSM ** -0.5),
        "b_ssm_glu": nrm(ks[12], (L, D_SSM), 0.01),
        "conv_w": nrm(ks[13], (L, CONV_WIDTH, D_CONV), CONV_WIDTH ** -0.5),
        "conv_b": nrm(ks[14], (L, D_CONV), 0.01),
        "conv_ln_g": gain(ks[15], (L, D_CONV)),
        "conv_ln_b": nrm(ks[16], (L, D_CONV), 0.01),
        "w_conv_pw2": nrm(ks[17], (L, D_CONV, D_CONV), D_CONV ** -0.5),
        "b_conv_pw2": nrm(ks[18], (L, D_CONV), 0.01),
        "g_attn_out": gain(ks[19], (L, D_ATTN)),
        "g_ssm_out": gain(ks[20], (L, D_SSM)),
        "g_conv_out": gain(ks[21], (L, D_CONV)),
        "w_out": nrm(ks[22], (L, D_MIX, D_MODEL), D_MIX ** -0.5),
        "norm_ffn_g": gain(ks[23], (L, D_MODEL)),
        "w_gate_up": nrm(ks[24], (L, D_MODEL, 2 * D_FF), D_MODEL ** -0.5),
        "w_down": nrm(ks[25], (L, D_FF, D_MODEL), D_FF ** -0.5),
        "final_norm_g": gain(ks[26], (D_MODEL,)),
    }


def reference(x, norm_mix_g, w_in, lambda_re, lambda_im, log_dt, b_re, b_im, c_re, c_im,
              d_skip, w_ssm_glu, b_ssm_glu, conv_w, conv_b, conv_ln_g, conv_ln_b,
              w_conv_pw2, b_conv_pw2, g_attn_out, g_ssm_out, g_conv_out, w_out,
              norm_ffn_g, w_gate_up, w_down, final_norm_g):
    bsz, seq, _ = x.shape
    cos, sin = rope_tables(seq)
    h = x
    for l in range(DEPTH):
        hn = rms_norm(h, norm_mix_g[l])
        proj = hn @ w_in[l]
        q, k, v, u_ssm, a_conv, g_conv = jnp.split(proj, SPLITS, axis=-1)
        qh = apply_partial_rope(split_heads(q, bsz, seq), cos, sin)
        kh = apply_partial_rope(split_heads(k, bsz, seq), cos, sin)
        vh = split_heads(v, bsz, seq)
        o_attn = moba_attention(qh, kh, vh).transpose(0, 2, 1, 3).reshape(bsz, seq, D_ATTN)
        o_ssm = s5_mixer(u_ssm, lambda_re[l], lambda_im[l], log_dt[l], b_re[l], b_im[l],
                         c_re[l], c_im[l], d_skip[l], w_ssm_glu[l], b_ssm_glu[l])
        o_conv = conformer_conv(a_conv, g_conv, conv_w[l], conv_b[l], conv_ln_g[l],
                                conv_ln_b[l], w_conv_pw2[l], b_conv_pw2[l])
        mixed = jnp.concatenate([rms_norm(o_attn, g_attn_out[l]),
                                 rms_norm(o_ssm, g_ssm_out[l]),
                                 rms_norm(o_conv, g_conv_out[l])], axis=-1)
        h = h + mixed @ w_out[l]
        hn = rms_norm(h, norm_ffn_g[l])
        gate, up = jnp.split(hn @ w_gate_up[l], 2, axis=-1)
        h = h + (jax.nn.silu(gate) * up) @ w_down[l]
    return rms_norm(h, final_norm_g)
```

```python
import functools
import math

import jax
import jax.numpy as jnp
from jax import lax
from jax.experimental import pallas as pl
from jax.experimental.pallas import tpu as pltpu

F32 = jnp.float32
BF16 = jnp.bfloat16

ATTN_HEADS = 8
HEAD_DIM = 64
D_ATTN = ATTN_HEADS * HEAD_DIM
SSM_GROUP = 16
SSM_GROUPS = 16
SSM_STATE = 64
D_SSM = SSM_GROUP * SSM_GROUPS
D_CONV = 256
CONV_WIDTH = 31
ROT_DIM = HEAD_DIM // 4
ROPE_THETA = 500000.0
MOBA_BLOCK = 256
MOBA_TOPK = 3
RMS_EPS = 1e-6
LN_EPS = 1e-5

LANES = 128
PAIR = 2 * HEAD_DIM
MASK_NEG = -1e30
VMEM_LIMIT = 56 * 1024 * 1024

TOK_TILE = 512
SSM_CHUNK = 64
CONV_ROWS = 64
CONV_HALO = 32


def _rms(x, g):
    return x * lax.rsqrt(jnp.mean(x * x, axis=-1, keepdims=True) + RMS_EPS) * g


def _sigmoid(x):
    return 1.0 / (1.0 + jnp.exp(-x))


def _gelu_tanh(x):
    c = math.sqrt(2.0 / math.pi)
    return 0.5 * x * (1.0 + jnp.tanh(c * (x + 0.044715 * (x * x * x))))


def _params(*sem):
    return pltpu.CompilerParams(dimension_semantics=sem, vmem_limit_bytes=VMEM_LIMIT)


def _inproj_kernel(h_ref, g_ref, w_ref, rc_ref, rs1_ref, rs2_ref,
                   q_ref, k_ref, v_ref, us_ref, uc_ref):
    hn = _rms(h_ref[0], g_ref[...]).astype(BF16)

    def proj(lo, hi):
        return jnp.dot(hn, w_ref[:, lo:hi], preferred_element_type=F32)

    rc, rs1, rs2 = rc_ref[...], rs1_ref[...], rs2_ref[...]

    def rope(t):
        return t * rc + pltpu.roll(t, LANES - ROT_DIM // 2, 1) * rs1 + pltpu.roll(t, ROT_DIM // 2, 1) * rs2

    for c in range(D_ATTN // LANES):
        lo = c * LANES
        q_ref[0, :, lo:lo + LANES] = (rope(proj(lo, lo + LANES)) * (HEAD_DIM ** -0.5)).astype(BF16)
        k_ref[0, :, lo:lo + LANES] = rope(proj(D_ATTN + lo, D_ATTN + lo + LANES)).astype(BF16)
    v_ref[0] = proj(2 * D_ATTN, 3 * D_ATTN).astype(BF16)
    o = 3 * D_ATTN
    us_ref[...] = proj(o, o + D_SSM)
    a = proj(o + D_SSM, o + D_SSM + D_CONV)
    g = proj(o + D_SSM + D_CONV, o + D_SSM + 2 * D_CONV)
    uc_ref[0] = a * _sigmoid(g)


def _inproj(h, g, w, rc, rs1, rs2):
    bsz, seq, d = h.shape
    tm = TOK_TILE
    d_in = w.shape[1]
    tok = lambda n: pl.BlockSpec((1, tm, n), lambda i, b: (b, i, 0))
    full = lambda r, c: pl.BlockSpec((r, c), lambda i, b: (0, 0))
    tab = pl.BlockSpec((tm, LANES), lambda i, b: (i, 0))
    return pl.pallas_call(
        _inproj_kernel,
        grid=(seq // tm, bsz),
        in_specs=[tok(d), full(1, d), full(d, d_in), tab, tab, tab],
        out_specs=[tok(D_ATTN), tok(D_ATTN), tok(D_ATTN),
                   pl.BlockSpec((tm, D_SSM), lambda i, b: (i, b)), tok(D_CONV)],
        out_shape=[jax.ShapeDtypeStruct((bsz, seq, D_ATTN), BF16)] * 3
        + [jax.ShapeDtypeStruct((seq, bsz * D_SSM), F32),
           jax.ShapeDtypeStruct((bsz, seq, D_CONV), F32)],
        compiler_params=_params("arbitrary", "arbitrary"),
        name="inproj",
    )(h, g, w, rc, rs1, rs2)


def _moba_kernel(q_ref, k_ref, v_ref, o_ref, kaug_ref, kmean_ref):
    i = pl.program_id(2)
    blk = MOBA_BLOCK
    nb = k_ref.shape[1] // blk

    @pl.when(i == 0)
    def _():
        lane = lax.broadcasted_iota(jnp.int32, (blk, LANES), 1)
        kmean_ref[...] = jnp.zeros_like(kmean_ref)
        for j in range(nb):
            kj = k_ref[0, j * blk:(j + 1) * blk, :]
            kaug_ref[j * blk:(j + 1) * blk, 0:LANES] = kj
            kaug_ref[j * blk:(j + 1) * blk, LANES:2 * LANES] = jnp.where(lane == j, 1.0, 0.0).astype(BF16)
            kmean_ref[j:j + 1, :] = jnp.mean(kj.astype(F32), axis=0, keepdims=True)

    qv = q_ref[0]
    lane = lax.broadcasted_iota(jnp.int32, (blk, LANES), 1)
    row = lax.broadcasted_iota(jnp.int32, (blk, blk), 0)
    col = lax.broadcasted_iota(jnp.int32, (blk, blk), 1)
    kmean = kmean_ref[...]
    km_hi = kmean.astype(BF16)
    km_lo = (kmean - km_hi.astype(F32)).astype(BF16)
    nt = (((1,), (1,)), ((), ()))
    off_i = pl.multiple_of(i * blk, blk)
    k_own = k_ref[0, pl.ds(off_i, blk), :]
    v_own = v_ref[0, pl.ds(off_i, blk), :]

    outs = []
    for head in range(2):
        in_head = (lane < HEAD_DIM) if head == 0 else (lane >= HEAD_DIM)
        qm = jnp.where(in_head, qv, jnp.zeros_like(qv))
        gate = (lax.dot_general(qm, km_hi, nt, preferred_element_type=F32)
                + lax.dot_general(qm, km_lo, nt, preferred_element_type=F32))
        valid = lane < i
        gm = jnp.where(valid, gate, -jnp.inf)
        ahead = jnp.zeros((blk, LANES), jnp.int32)
        for n in range(nb - 1):
            gn = gm[:, n:n + 1]
            beats = (gn > gm) | ((gn == gm) & (lane > n))
            ahead = ahead + jnp.where(beats, 1, 0)
        sel = valid & (ahead < MOBA_TOPK)
        pen = jnp.where(sel, 0.0, MASK_NEG).astype(BF16)
        q_aug = jnp.concatenate([qm, pen], axis=1)

        s = lax.dot_general(qm, k_own, nt, preferred_element_type=F32)
        s = jnp.where(col <= row, s, -jnp.inf)
        m = jnp.max(s, axis=1, keepdims=True)
        p = jnp.exp(s - m)
        l = jnp.sum(p, axis=1, keepdims=True)
        acc = jnp.dot(p.astype(BF16), v_own, preferred_element_type=F32)

        def body(j, carry):
            m, l, acc = carry
            off = pl.multiple_of(j * blk, blk)
            kj = kaug_ref[pl.ds(off, blk), :]
            vj = v_ref[0, pl.ds(off, blk), :]
            s = lax.dot_general(q_aug, kj, nt, preferred_element_type=F32)
            m_new = jnp.maximum(m, jnp.max(s, axis=1, keepdims=True))
            alpha = jnp.exp(m - m_new)
            p = jnp.exp(s - m_new)
            l = alpha * l + jnp.sum(p, axis=1, keepdims=True)
            acc = alpha * acc + jnp.dot(p.astype(BF16), vj, preferred_element_type=F32)
            return m_new, l, acc

        m, l, acc = lax.fori_loop(0, i, body, (m, l, acc))
        outs.append(acc / l)

    o_ref[0] = jnp.where(lane < HEAD_DIM, outs[0], outs[1]).astype(o_ref.dtype)


def _moba(q, k, v):
    bsz, seq, _ = q.shape
    blk = MOBA_BLOCK
    nb = seq // blk
    assert nb <= LANES
    return pl.pallas_call(
        _moba_kernel,
        grid=(bsz, D_ATTN // PAIR, nb),
        in_specs=[pl.BlockSpec((1, blk, PAIR), lambda b, hp, i: (b, i, hp)),
                  pl.BlockSpec((1, seq, PAIR), lambda b, hp, i: (b, 0, hp)),
                  pl.BlockSpec((1, seq, PAIR), lambda b, hp, i: (b, 0, hp))],
        out_specs=pl.BlockSpec((1, blk, PAIR), lambda b, hp, i: (b, i, hp)),
        out_shape=jax.ShapeDtypeStruct((bsz, seq, D_ATTN), BF16),
        scratch_shapes=[pltpu.VMEM((seq, 2 * LANES), BF16), pltpu.VMEM((LANES, LANES), F32)],
        compiler_params=_params("arbitrary", "arbitrary", "arbitrary"),
        name="moba",
    )(q, k, v)


def _s5_kernel(u_ref, wb_ref, are_ref, aim_ref, wc_ref, d_ref, wg_ref, bg_ref, gn_ref,
               o_ref, xh_ref, st_ref, *, bsz):
    n = SSM_GROUPS * SSM_STATE

    @pl.when(pl.program_id(0) == 0)
    def _():
        st_ref[...] = jnp.zeros_like(st_ref)

    u = u_ref[...]
    xh_ref[...] = jnp.dot(u.astype(BF16), wb_ref[...], preferred_element_type=F32)
    ar, ai = are_ref[...], aim_ref[...]

    def step(t, carry):
        hr, hi = carry
        r0 = pl.multiple_of(t * bsz, bsz)
        xr = xh_ref[pl.ds(r0, bsz), 0:n]
        xi = xh_ref[pl.ds(r0, bsz), n:2 * n]
        nr = ar * hr - ai * hi + xr
        ni = ar * hi + ai * hr + xi
        xh_ref[pl.ds(r0, bsz), 0:n] = nr
        xh_ref[pl.ds(r0, bsz), n:2 * n] = ni
        return nr, ni

    hr, hi = lax.fori_loop(0, u.shape[0] // bsz, step, (st_ref[:, 0:n], st_ref[:, n:2 * n]))
    st_ref[:, 0:n] = hr
    st_ref[:, n:2 * n] = hi

    y = jnp.dot(xh_ref[...].astype(BF16), wc_ref[...], preferred_element_type=F32) + d_ref[...] * u
    z = _gelu_tanh(y)
    gl = jnp.dot(z.astype(BF16), wg_ref[...], preferred_element_type=F32) + bg_ref[...]
    o_ref[...] = _rms(z * _sigmoid(gl), gn_ref[...]).astype(o_ref.dtype)


def _s5(u, wb, a_re, a_im, wc, dsk, wg, bg, gn, bsz):
    rows_total = u.shape[0]
    rows = SSM_CHUNK * bsz
    n2 = 2 * SSM_GROUPS * SSM_STATE
    full = lambda r, c: pl.BlockSpec((r, c), lambda t: (0, 0))
    return pl.pallas_call(
        functools.partial(_s5_kernel, bsz=bsz),
        grid=(rows_total // rows,),
        in_specs=[pl.BlockSpec((rows, D_SSM), lambda t: (t, 0)),
                  full(D_SSM, n2), full(bsz, n2 // 2), full(bsz, n2 // 2), full(n2, D_SSM),
                  full(1, D_SSM), full(D_SSM, D_SSM), full(1, D_SSM), full(1, D_SSM)],
        out_specs=pl.BlockSpec((rows, D_SSM), lambda t: (t, 0)),
        out_shape=jax.ShapeDtypeStruct((rows_total, D_SSM), BF16),
        scratch_shapes=[pltpu.VMEM((rows, n2), F32), pltpu.VMEM((bsz, n2), F32)],
        compiler_params=_params("arbitrary"),
        name="s5",
    )(u, wb, a_re, a_im, wc, dsk, wg, bg, gn)


def _s5_matrices(lam_re, lam_im, log_dt, b_re, b_im, c_re, c_im, bsz):
    dt = jnp.exp(log_dt)[:, None]
    mag = jnp.exp(lam_re * dt)
    ab_re = mag * jnp.cos(lam_im * dt)
    ab_im = mag * jnp.sin(lam_im * dt)
    nr = ab_re - 1.0
    den = lam_re * lam_re + lam_im * lam_im
    f_re = (nr * lam_re + ab_im * lam_im) / den
    f_im = (ab_im * lam_re - nr * lam_im) / den
    bb_re = f_re[..., None] * b_re - f_im[..., None] * b_im
    bb_im = f_re[..., None] * b_im + f_im[..., None] * b_re
    eye = jnp.eye(SSM_GROUPS, dtype=F32)
    n = SSM_GROUPS * SSM_STATE
    wb = lambda bb: jnp.einsum('gpc,gh->gchp', bb, eye).reshape(D_SSM, n)
    wb_all = jnp.concatenate([wb(bb_re), wb(bb_im)], axis=1).astype(BF16)
    wc = lambda cc: jnp.einsum('gcp,gh->gphc', cc, eye).reshape(n, D_SSM)
    wc_all = jnp.concatenate([wc(c_re), -wc(c_im)], axis=0).astype(BF16)
    a_re = jnp.broadcast_to(ab_re.reshape(1, n), (bsz, n))
    a_im = jnp.broadcast_to(ab_im.reshape(1, n), (bsz, n))
    return wb_all, a_re, a_im, wc_all


def _conv_kernel(u_ref, halo_ref, w_ref, cb_ref, lg_ref, lb_ref, pw_ref, pb_ref, gn_ref,
                 o_ref, xx_ref, y_ref):
    tm = u_ref.shape[1]
    first = pl.program_id(1) == 0
    xx_ref[0:CONV_HALO, :] = jnp.where(first, 0.0, halo_ref[0])
    xx_ref[CONV_HALO:, :] = u_ref[0]
    base = CONV_HALO - (CONV_WIDTH - 1)
    for r in range(0, tm, CONV_ROWS):
        acc = jnp.broadcast_to(cb_ref[...], (CONV_ROWS, D_CONV))
        for j in range(CONV_WIDTH):
            acc = acc + w_ref[j:j + 1, :] * xx_ref[r + base + j:r + base + j + CONV_ROWS, :]
        y_ref[r:r + CONV_ROWS, :] = acc
    y = y_ref[...]
    mu = jnp.mean(y, axis=-1, keepdims=True)
    yc = y - mu
    yn = yc * lax.rsqrt(jnp.mean(yc * yc, axis=-1, keepdims=True) + LN_EPS) * lg_ref[...] + lb_ref[...]
    a = yn * _sigmoid(yn)
    out = jnp.dot(a.astype(BF16), pw_ref[...], preferred_element_type=F32) + pb_ref[...]
    o_ref[0] = _rms(out, gn_ref[...]).astype(o_ref.dtype)


def _conv(u, w, cb, lg, lb, pw, pb, gn):
    bsz, seq, _ = u.shape
    tm = TOK_TILE
    per = tm // CONV_HALO
    full = lambda r, c: pl.BlockSpec((r, c), lambda b, i: (0, 0))
    return pl.pallas_call(
        _conv_kernel,
        grid=(bsz, seq // tm),
        in_specs=[pl.BlockSpec((1, tm, D_CONV), lambda b, i: (b, i, 0)),
                  pl.BlockSpec((1, CONV_HALO, D_CONV), lambda b, i: (b, jnp.maximum(i * per - 1, 0), 0)),
                  full(CONV_HALO, D_CONV), full(1, D_CONV), full(1, D_CONV), full(1, D_CONV),
                  full(D_CONV, D_CONV), full(1, D_CONV), full(1, D_CONV)],
        out_specs=pl.BlockSpec((1, tm, D_CONV), lambda b, i: (b, i, 0)),
        out_shape=jax.ShapeDtypeStruct((bsz, seq, D_CONV), BF16),
        scratch_shapes=[pltpu.VMEM((tm + CONV_HALO, D_CONV), F32), pltpu.VMEM((tm, D_CONV), F32)],
        compiler_params=_params("arbitrary", "arbitrary"),
        name="conv",
    )(u, u, w, cb, lg, lb, pw, pb, gn)


def _mix_ffn_kernel(oa_ref, os_ref, oc_ref, h_ref, ga_ref, wo_ref, gf_ref, wgu_ref, wd_ref, gfin_ref,
                    o_ref, act_ref, *, d_ff, ff_chunk, final):
    an = _rms(oa_ref[0].astype(F32), ga_ref[...]).astype(BF16)
    mixed = jnp.concatenate([an, os_ref[...], oc_ref[0]], axis=1)
    h1 = h_ref[0] + jnp.dot(mixed, wo_ref[...], preferred_element_type=F32)
    hn = _rms(h1, gf_ref[...]).astype(BF16)
    for c in range(0, d_ff, ff_chunk):
        gate = jnp.dot(hn, wgu_ref[:, c:c + ff_chunk], preferred_element_type=F32)
        up = jnp.dot(hn, wgu_ref[:, d_ff + c:d_ff + c + ff_chunk], preferred_element_type=F32)
        act_ref[:, c:c + ff_chunk] = (gate * _sigmoid(gate) * up).astype(BF16)
    h2 = h1 + jnp.dot(act_ref[...], wd_ref[...], preferred_element_type=F32)
    if final:
        h2 = _rms(h2, gfin_ref[...])
    o_ref[0] = h2


def _mix_ffn(oa, os2, oc, h, ga, wo, gf, wgu, wd, gfin, final):
    bsz, seq, d = h.shape
    tm = TOK_TILE
    d_ff = wd.shape[0]
    tok = lambda n: pl.BlockSpec((1, tm, n), lambda i, b: (b, i, 0))
    full = lambda r, c: pl.BlockSpec((r, c), lambda i, b: (0, 0), pipeline_mode=pl.Buffered(1))
    return pl.pallas_call(
        functools.partial(_mix_ffn_kernel, d_ff=d_ff, ff_chunk=256, final=final),
        grid=(seq // tm, bsz),
        in_specs=[tok(D_ATTN), pl.BlockSpec((tm, D_SSM), lambda i, b: (i, b)), tok(D_CONV), tok(d),
                  full(1, D_ATTN), full(d, d), full(1, d), full(d, 2 * d_ff), full(d_ff, d), full(1, d)],
        out_specs=tok(d),
        out_shape=jax.ShapeDtypeStruct((bsz, seq, d), F32),
        scratch_shapes=[pltpu.VMEM((tm, d_ff), BF16)],
        compiler_params=_params("arbitrary", "arbitrary"),
        name="mix_ffn",
    )(oa, os2, oc, h, ga, wo, gf, wgu, wd, gfin)


def _rope_tables(seq):
    half = ROT_DIM // 2
    inv = jnp.power(ROPE_THETA, -jnp.arange(0, ROT_DIM, 2, dtype=F32) / ROT_DIM)
    ang = jnp.arange(seq, dtype=F32)[:, None] * inv[None, :]
    cos, sin = jnp.cos(ang), jnp.sin(ang)
    ones = jnp.ones((seq, HEAD_DIM - ROT_DIM), F32)
    zeros = jnp.zeros((seq, HEAD_DIM - ROT_DIM), F32)
    zh = jnp.zeros((seq, half), F32)
    reps = LANES // HEAD_DIM
    rc = jnp.tile(jnp.concatenate([cos, cos, ones], axis=1), (1, reps))
    rs1 = jnp.tile(jnp.concatenate([-sin, zh, zeros], axis=1), (1, reps))
    rs2 = jnp.tile(jnp.concatenate([zh, sin, zeros], axis=1), (1, reps))
    return rc, rs1, rs2


def kernel(x, norm_mix_g, w_in, lambda_re, lambda_im, log_dt, b_re, b_im, c_re, c_im, d_skip, w_ssm_glu, b_ssm_glu, conv_w, conv_b, conv_ln_g, conv_ln_b, w_conv_pw2, b_conv_pw2, g_attn_out, g_ssm_out, g_conv_out, w_out, norm_ffn_g, w_gate_up, w_down, final_norm_g):
    bsz, seq, d = x.shape
    depth = w_in.shape[0]
    rc, rs1, rs2 = _rope_tables(seq)
    row = lambda a: a.reshape(1, -1)
    h = x
    for l in range(depth):
        q, k, v, u_ssm, u_conv = _inproj(h, row(norm_mix_g[l]), w_in[l].astype(BF16), rc, rs1, rs2)
        o_attn = _moba(q, k, v)
        wb, a_re, a_im, wc = _s5_matrices(lambda_re[l], lambda_im[l], log_dt[l], b_re[l], b_im[l],
                                          c_re[l], c_im[l], bsz)
        o_ssm = _s5(u_ssm.reshape(seq * bsz, D_SSM), wb, a_re, a_im, wc, row(d_skip[l]),
                    w_ssm_glu[l].astype(BF16), row(b_ssm_glu[l]), row(g_ssm_out[l]), bsz)
        cw = jnp.pad(conv_w[l], ((0, CONV_HALO - CONV_WIDTH), (0, 0)))
        o_conv = _conv(u_conv, cw, row(conv_b[l]), row(conv_ln_g[l]), row(conv_ln_b[l]),
                       w_conv_pw2[l].astype(BF16), row(b_conv_pw2[l]), row(g_conv_out[l]))
        h = _mix_ffn(o_attn, o_ssm.reshape(seq, bsz * D_SSM), o_conv, h, row(g_attn_out[l]),
                     w_out[l].astype(BF16), row(norm_ffn_g[l]), w_gate_up[l].astype(BF16),
                     w_down[l].astype(BF16), row(final_norm_g), final=(l == depth - 1))
    return h
```

```python
import functools
import math

import jax
import jax.numpy as jnp
from jax import lax
from jax.experimental import pallas as pl
from jax.experimental.pallas import tpu as pltpu

F32 = jnp.float32
BF16 = jnp.bfloat16

ATTN_HEADS = 8
HEAD_DIM = 64
D_ATTN = ATTN_HEADS * HEAD_DIM
SSM_GROUP = 16
SSM_GROUPS = 16
SSM_STATE = 64
D_SSM = SSM_GROUP * SSM_GROUPS
D_CONV = 256
CONV_WIDTH = 31
ROT_DIM = HEAD_DIM // 4
ROPE_THETA = 500000.0
MOBA_BLOCK = 256
MOBA_TOPK = 3
RMS_EPS = 1e-6
LN_EPS = 1e-5

LANES = 128
PAIR = 2 * HEAD_DIM
MASK_NEG = -1e30
VMEM_LIMIT = 56 * 1024 * 1024

TOK_TILE = 512
SSM_CHUNK = 64
CONV_ROWS = 64
CONV_HALO = 32


def _rms(x, g):
    return x * lax.rsqrt(jnp.mean(x * x, axis=-1, keepdims=True) + RMS_EPS) * g


def _sigmoid(x):
    return 1.0 / (1.0 + jnp.exp(-x))


def _gelu_tanh(x):
    c = math.sqrt(2.0 / math.pi)
    return 0.5 * x * (1.0 + jnp.tanh(c * (x + 0.044715 * (x * x * x))))


def _params(*sem):
    return pltpu.CompilerParams(dimension_semantics=sem, vmem_limit_bytes=VMEM_LIMIT)


def _inproj_kernel(h_ref, g_ref, w_ref, rc_ref, rs1_ref, rs2_ref,
                   q_ref, k_ref, v_ref, us_ref, uc_ref):
    hn = _rms(h_ref[0], g_ref[...]).astype(BF16)

    def proj(lo, hi):
        return jnp.dot(hn, w_ref[:, lo:hi], preferred_element_type=F32)

    rc, rs1, rs2 = rc_ref[...], rs1_ref[...], rs2_ref[...]

    def rope(t):
        return t * rc + pltpu.roll(t, LANES - ROT_DIM // 2, 1) * rs1 + pltpu.roll(t, ROT_DIM // 2, 1) * rs2

    for c in range(D_ATTN // LANES):
        lo = c * LANES
        q_ref[0, :, lo:lo + LANES] = (rope(proj(lo, lo + LANES)) * (HEAD_DIM ** -0.5)).astype(BF16)
        k_ref[0, :, lo:lo + LANES] = rope(proj(D_ATTN + lo, D_ATTN + lo + LANES)).astype(BF16)
    v_ref[0] = proj(2 * D_ATTN, 3 * D_ATTN).astype(BF16)
    o = 3 * D_ATTN
    us_ref[...] = proj(o, o + D_SSM)
    a = proj(o + D_SSM, o + D_SSM + D_CONV)
    g = proj(o + D_SSM + D_CONV, o + D_SSM + 2 * D_CONV)
    uc_ref[0] = a * _sigmoid(g)


def _inproj(h, g, w, rc, rs1, rs2):
    bsz, seq, d = h.shape
    tm = TOK_TILE
    d_in = w.shape[1]
    tok = lambda n: pl.BlockSpec((1, tm, n), lambda i, b: (b, i, 0))
    full = lambda r, c: pl.BlockSpec((r, c), lambda i, b: (0, 0))
    tab = pl.BlockSpec((tm, LANES), lambda i, b: (i, 0))
    return pl.pallas_call(
        _inproj_kernel,
        grid=(seq // tm, bsz),
        in_specs=[tok(d), full(1, d), full(d, d_in), tab, tab, tab],
        out_specs=[tok(D_ATTN), tok(D_ATTN), tok(D_ATTN),
                   pl.BlockSpec((tm, D_SSM), lambda i, b: (i, b)), tok(D_CONV)],
        out_shape=[jax.ShapeDtypeStruct((bsz, seq, D_ATTN), BF16)] * 3
        + [jax.ShapeDtypeStruct((seq, bsz * D_SSM), F32),
           jax.ShapeDtypeStruct((bsz, seq, D_CONV), F32)],
        compiler_params=_params("arbitrary", "arbitrary"),
        name="inproj",
    )(h, g, w, rc, rs1, rs2)


def _moba_kernel(q_ref, k_ref, v_ref, o_ref, kaug_ref, vt_ref, kmean_ref):
    i = pl.program_id(2)
    blk = MOBA_BLOCK
    nb = kaug_ref.shape[0]
    nbp = kmean_ref.shape[0]

    @pl.when(i == 0)
    def _():
        lane = lax.broadcasted_iota(jnp.int32, (blk, LANES), 1)
        kmean_ref[...] = jnp.zeros_like(kmean_ref)
        for j in range(nb):
            kj = k_ref[0, j * blk:(j + 1) * blk, :]
            kaug_ref[j, :, 0:LANES] = kj
            kaug_ref[j, :, LANES:2 * LANES] = jnp.where(lane == j, 1.0, 0.0).astype(BF16)
            kmean_ref[j:j + 1, :] = jnp.mean(kj.astype(F32), axis=0, keepdims=True)
            vt_ref[j] = v_ref[0, j * blk:(j + 1) * blk, :].astype(F32).T.astype(BF16)

    qt = q_ref[0].astype(F32).T.astype(BF16)
    dim = lax.broadcasted_iota(jnp.int32, (PAIR, blk), 0)
    cand = lax.broadcasted_iota(jnp.int32, (nbp, blk), 0)
    key = lax.broadcasted_iota(jnp.int32, (blk, blk), 0)
    qry = lax.broadcasted_iota(jnp.int32, (blk, blk), 1)
    kmean = kmean_ref[...]
    km_hi = kmean.astype(BF16)
    km_lo = (kmean - km_hi.astype(F32)).astype(BF16)
    k_own = k_ref[0, pl.ds(pl.multiple_of(i * blk, blk), blk), :]
    pad = jnp.zeros((LANES - nbp, blk), BF16)

    q_augs, init = [], []
    for head in range(2):
        in_head = (dim < HEAD_DIM) if head == 0 else (dim >= HEAD_DIM)
        qm = jnp.where(in_head, qt, jnp.zeros_like(qt))
        gate = (jnp.dot(km_hi, qm, preferred_element_type=F32)
                + jnp.dot(km_lo, qm, preferred_element_type=F32))
        valid = cand < i
        gm = jnp.where(valid, gate, -jnp.inf)
        ahead = jnp.zeros((nbp, blk), jnp.int32)
        for n in range(nb - 1):
            gn = gm[n:n + 1, :]
            beats = (gn > gm) | ((gn == gm) & (cand > n))
            ahead = ahead + jnp.where(beats, 1, 0)
        sel = valid & (ahead < MOBA_TOPK)
        pen = jnp.where(sel, 0.0, MASK_NEG).astype(BF16)
        q_augs.append(jnp.concatenate([qm, pen, pad], axis=0))

        s = jnp.dot(k_own, qm, preferred_element_type=F32)
        s = jnp.where(key <= qry, s, -jnp.inf)
        m = jnp.max(s, axis=0, keepdims=True)
        p = jnp.exp(s - m)
        l = jnp.sum(p, axis=0, keepdims=True)
        acc = jnp.dot(vt_ref[i], p.astype(BF16), preferred_element_type=F32)
        init += [m, l, acc]

    def body(jj, carry):
        j0 = 2 * jj
        j1 = j0 + 1
        out = []
        for head in range(2):
            m, l, acc = carry[3 * head:3 * head + 3]
            s0 = jnp.dot(kaug_ref[j0], q_augs[head], preferred_element_type=F32)
            s1 = jnp.dot(kaug_ref[j1], q_augs[head], preferred_element_type=F32)
            m_new = jnp.maximum(m, jnp.max(jnp.maximum(s0, s1), axis=0, keepdims=True))
            alpha = jnp.exp(m - m_new)
            p0 = jnp.exp(s0 - m_new)
            p1 = jnp.exp(s1 - m_new)
            l = alpha * l + jnp.sum(p0 + p1, axis=0, keepdims=True)
            acc = (alpha * acc + jnp.dot(vt_ref[j0], p0.astype(BF16), preferred_element_type=F32)
                   + jnp.dot(vt_ref[j1], p1.astype(BF16), preferred_element_type=F32))
            out += [m_new, l, acc]
        return tuple(out)

    res = lax.fori_loop(0, (i + 1) // 2, body, tuple(init))
    o_a = res[2] * (1.0 / res[1])
    o_b = res[5] * (1.0 / res[4])
    o_ref[0] = jnp.where(dim < HEAD_DIM, o_a, o_b).T.astype(o_ref.dtype)


def _moba(q, k, v):
    bsz, seq, _ = q.shape
    blk = MOBA_BLOCK
    nb = seq // blk
    nbp = -(-nb // 16) * 16
    assert nbp <= LANES
    return pl.pallas_call(
        _moba_kernel,
        grid=(bsz, D_ATTN // PAIR, nb),
        in_specs=[pl.BlockSpec((1, blk, PAIR), lambda b, hp, i: (b, i, hp)),
                  pl.BlockSpec((1, seq, PAIR), lambda b, hp, i: (b, 0, hp)),
                  pl.BlockSpec((1, seq, PAIR), lambda b, hp, i: (b, 0, hp))],
        out_specs=pl.BlockSpec((1, blk, PAIR), lambda b, hp, i: (b, i, hp)),
        out_shape=jax.ShapeDtypeStruct((bsz, seq, D_ATTN), BF16),
        scratch_shapes=[pltpu.VMEM((nb, blk, 2 * LANES), BF16), pltpu.VMEM((nb, PAIR, blk), BF16),
                        pltpu.VMEM((nbp, LANES), F32)],
        compiler_params=_params("arbitrary", "arbitrary", "arbitrary"),
        name="moba",
    )(q, k, v)


def _s5_kernel(u_ref, wb_ref, are_ref, aim_ref, wc_ref, d_ref, wg_ref, bg_ref, gn_ref,
               o_ref, xh_ref, st_ref, *, bsz):
    n = SSM_GROUPS * SSM_STATE

    @pl.when(pl.program_id(0) == 0)
    def _():
        st_ref[...] = jnp.zeros_like(st_ref)

    u = u_ref[...]
    xh_ref[...] = jnp.dot(u.astype(BF16), wb_ref[...], preferred_element_type=F32)
    ar, ai = are_ref[...], aim_ref[...]

    def step(t, carry):
        hr, hi = carry
        r0 = pl.multiple_of(t * bsz, bsz)
        xr = xh_ref[pl.ds(r0, bsz), 0:n]
        xi = xh_ref[pl.ds(r0, bsz), n:2 * n]
        nr = ar * hr - ai * hi + xr
        ni = ar * hi + ai * hr + xi
        xh_ref[pl.ds(r0, bsz), 0:n] = nr
        xh_ref[pl.ds(r0, bsz), n:2 * n] = ni
        return nr, ni

    hr, hi = lax.fori_loop(0, u.shape[0] // bsz, step, (st_ref[:, 0:n], st_ref[:, n:2 * n]))
    st_ref[:, 0:n] = hr
    st_ref[:, n:2 * n] = hi

    y = jnp.dot(xh_ref[...].astype(BF16), wc_ref[...], preferred_element_type=F32) + d_ref[...] * u
    z = _gelu_tanh(y)
    gl = jnp.dot(z.astype(BF16), wg_ref[...], preferred_element_type=F32) + bg_ref[...]
    o_ref[...] = _rms(z * _sigmoid(gl), gn_ref[...]).astype(o_ref.dtype)


def _s5(u, wb, a_re, a_im, wc, dsk, wg, bg, gn, bsz):
    rows_total = u.shape[0]
    rows = SSM_CHUNK * bsz
    n2 = 2 * SSM_GROUPS * SSM_STATE
    full = lambda r, c: pl.BlockSpec((r, c), lambda t: (0, 0))
    return pl.pallas_call(
        functools.partial(_s5_kernel, bsz=bsz),
        grid=(rows_total // rows,),
        in_specs=[pl.BlockSpec((rows, D_SSM), lambda t: (t, 0)),
                  full(D_SSM, n2), full(bsz, n2 // 2), full(bsz, n2 // 2), full(n2, D_SSM),
                  full(1, D_SSM), full(D_SSM, D_SSM), full(1, D_SSM), full(1, D_SSM)],
        out_specs=pl.BlockSpec((rows, D_SSM), lambda t: (t, 0)),
        out_shape=jax.ShapeDtypeStruct((rows_total, D_SSM), BF16),
        scratch_shapes=[pltpu.VMEM((rows, n2), F32), pltpu.VMEM((bsz, n2), F32)],
        compiler_params=_params("arbitrary"),
        name="s5",
    )(u, wb, a_re, a_im, wc, dsk, wg, bg, gn)


def _s5_matrices(lam_re, lam_im, log_dt, b_re, b_im, c_re, c_im, bsz):
    dt = jnp.exp(log_dt)[:, None]
    mag = jnp.exp(lam_re * dt)
    ab_re = mag * jnp.cos(lam_im * dt)
    ab_im = mag * jnp.sin(lam_im * dt)
    nr = ab_re - 1.0
    den = lam_re * lam_re + lam_im * lam_im
    f_re = (nr * lam_re + ab_im * lam_im) / den
    f_im = (ab_im * lam_re - nr * lam_im) / den
    bb_re = f_re[..., None] * b_re - f_im[..., None] * b_im
    bb_im = f_re[..., None] * b_im + f_im[..., None] * b_re
    eye = jnp.eye(SSM_GROUPS, dtype=F32)
    n = SSM_GROUPS * SSM_STATE
    wb = lambda bb: jnp.einsum('gpc,gh->gchp', bb, eye).reshape(D_SSM, n)
    wb_all = jnp.concatenate([wb(bb_re), wb(bb_im)], axis=1).astype(BF16)
    wc = lambda cc: jnp.einsum('gcp,gh->gphc', cc, eye).reshape(n, D_SSM)
    wc_all = jnp.concatenate([wc(c_re), -wc(c_im)], axis=0).astype(BF16)
    a_re = jnp.broadcast_to(ab_re.reshape(1, n), (bsz, n))
    a_im = jnp.broadcast_to(ab_im.reshape(1, n), (bsz, n))
    return wb_all, a_re, a_im, wc_all


def _conv_kernel(u_ref, halo_ref, w_ref, cb_ref, lg_ref, lb_ref, pw_ref, pb_ref, gn_ref,
                 o_ref, xx_ref, y_ref):
    tm = u_ref.shape[1]
    first = pl.program_id(1) == 0
    xx_ref[0:CONV_HALO, :] = jnp.where(first, 0.0, halo_ref[0])
    xx_ref[CONV_HALO:, :] = u_ref[0]
    base = CONV_HALO - (CONV_WIDTH - 1)
    for r in range(0, tm, CONV_ROWS):
        acc = jnp.broadcast_to(cb_ref[...], (CONV_ROWS, D_CONV))
        for j in range(CONV_WIDTH):
            acc = acc + w_ref[j:j + 1, :] * xx_ref[r + base + j:r + base + j + CONV_ROWS, :]
        y_ref[r:r + CONV_ROWS, :] = acc
    y = y_ref[...]
    mu = jnp.mean(y, axis=-1, keepdims=True)
    yc = y - mu
    yn = yc * lax.rsqrt(jnp.mean(yc * yc, axis=-1, keepdims=True) + LN_EPS) * lg_ref[...] + lb_ref[...]
    a = yn * _sigmoid(yn)
    out = jnp.dot(a.astype(BF16), pw_ref[...], preferred_element_type=F32) + pb_ref[...]
    o_ref[0] = _rms(out, gn_ref[...]).astype(o_ref.dtype)


def _conv(u, w, cb, lg, lb, pw, pb, gn):
    bsz, seq, _ = u.shape
    tm = TOK_TILE
    per = tm // CONV_HALO
    full = lambda r, c: pl.BlockSpec((r, c), lambda b, i: (0, 0))
    return pl.pallas_call(
        _conv_kernel,
        grid=(bsz, seq // tm),
        in_specs=[pl.BlockSpec((1, tm, D_CONV), lambda b, i: (b, i, 0)),
                  pl.BlockSpec((1, CONV_HALO, D_CONV), lambda b, i: (b, jnp.maximum(i * per - 1, 0), 0)),
                  full(CONV_HALO, D_CONV), full(1, D_CONV), full(1, D_CONV), full(1, D_CONV),
                  full(D_CONV, D_CONV), full(1, D_CONV), full(1, D_CONV)],
        out_specs=pl.BlockSpec((1, tm, D_CONV), lambda b, i: (b, i, 0)),
        out_shape=jax.ShapeDtypeStruct((bsz, seq, D_CONV), BF16),
        scratch_shapes=[pltpu.VMEM((tm + CONV_HALO, D_CONV), F32), pltpu.VMEM((tm, D_CONV), F32)],
        compiler_params=_params("arbitrary", "arbitrary"),
        name="conv",
    )(u, u, w, cb, lg, lb, pw, pb, gn)


def _mix_ffn_kernel(oa_ref, os_ref, oc_ref, h_ref, ga_ref, wo_ref, gf_ref, wgu_ref, wd_ref, gfin_ref,
                    o_ref, act_ref, *, d_ff, ff_chunk, final):
    an = _rms(oa_ref[0].astype(F32), ga_ref[...]).astype(BF16)
    mixed = jnp.concatenate([an, os_ref[...], oc_ref[0]], axis=1)
    h1 = h_ref[0] + jnp.dot(mixed, wo_ref[...], preferred_element_type=F32)
    hn = _rms(h1, gf_ref[...]).astype(BF16)
    for c in range(0, d_ff, ff_chunk):
        gate = jnp.dot(hn, wgu_ref[:, c:c + ff_chunk], preferred_element_type=F32)
        up = jnp.dot(hn, wgu_ref[:, d_ff + c:d_ff + c + ff_chunk], preferred_element_type=F32)
        act_ref[:, c:c + ff_chunk] = (gate * _sigmoid(gate) * up).astype(BF16)
    h2 = h1 + jnp.dot(act_ref[...], wd_ref[...], preferred_element_type=F32)
    if final:
        h2 = _rms(h2, gfin_ref[...])
    o_ref[0] = h2


def _mix_ffn(oa, os2, oc, h, ga, wo, gf, wgu, wd, gfin, final):
    bsz, seq, d = h.shape
    tm = TOK_TILE
    d_ff = wd.shape[0]
    tok = lambda n: pl.BlockSpec((1, tm, n), lambda i, b: (b, i, 0))
    full = lambda r, c: pl.BlockSpec((r, c), lambda i, b: (0, 0), pipeline_mode=pl.Buffered(1))
    return pl.pallas_call(
        functools.partial(_mix_ffn_kernel, d_ff=d_ff, ff_chunk=256, final=final),
        grid=(seq // tm, bsz),
        in_specs=[tok(D_ATTN), pl.BlockSpec((tm, D_SSM), lambda i, b: (i, b)), tok(D_CONV), tok(d),
                  full(1, D_ATTN), full(d, d), full(1, d), full(d, 2 * d_ff), full(d_ff, d), full(1, d)],
        out_specs=tok(d),
        out_shape=jax.ShapeDtypeStruct((bsz, seq, d), F32),
        scratch_shapes=[pltpu.VMEM((tm, d_ff), BF16)],
        compiler_params=_params("arbitrary", "arbitrary"),
        name="mix_ffn",
    )(oa, os2, oc, h, ga, wo, gf, wgu, wd, gfin)


def _rope_tables(seq):
    half = ROT_DIM // 2
    inv = jnp.power(ROPE_THETA, -jnp.arange(0, ROT_DIM, 2, dtype=F32) / ROT_DIM)
    ang = jnp.arange(seq, dtype=F32)[:, None] * inv[None, :]
    cos, sin = jnp.cos(ang), jnp.sin(ang)
    ones = jnp.ones((seq, HEAD_DIM - ROT_DIM), F32)
    zeros = jnp.zeros((seq, HEAD_DIM - ROT_DIM), F32)
    zh = jnp.zeros((seq, half), F32)
    reps = LANES // HEAD_DIM
    rc = jnp.tile(jnp.concatenate([cos, cos, ones], axis=1), (1, reps))
    rs1 = jnp.tile(jnp.concatenate([-sin, zh, zeros], axis=1), (1, reps))
    rs2 = jnp.tile(jnp.concatenate([zh, sin, zeros], axis=1), (1, reps))
    return rc, rs1, rs2


def kernel(x, norm_mix_g, w_in, lambda_re, lambda_im, log_dt, b_re, b_im, c_re, c_im, d_skip, w_ssm_glu, b_ssm_glu, conv_w, conv_b, conv_ln_g, conv_ln_b, w_conv_pw2, b_conv_pw2, g_attn_out, g_ssm_out, g_conv_out, w_out, norm_ffn_g, w_gate_up, w_down, final_norm_g):
    bsz, seq, d = x.shape
    depth = w_in.shape[0]
    rc, rs1, rs2 = _rope_tables(seq)
    row = lambda a: a.reshape(1, -1)
    h = x
    for l in range(depth):
        q, k, v, u_ssm, u_conv = _inproj(h, row(norm_mix_g[l]), w_in[l].astype(BF16), rc, rs1, rs2)
        o_attn = _moba(q, k, v)
        wb, a_re, a_im, wc = _s5_matrices(lambda_re[l], lambda_im[l], log_dt[l], b_re[l], b_im[l],
                                          c_re[l], c_im[l], bsz)
        o_ssm = _s5(u_ssm.reshape(seq * bsz, D_SSM), wb, a_re, a_im, wc, row(d_skip[l]),
                    w_ssm_glu[l].astype(BF16), row(b_ssm_glu[l]), row(g_ssm_out[l]), bsz)
        cw = jnp.pad(conv_w[l], ((0, CONV_HALO - CONV_WIDTH), (0, 0)))
        o_conv = _conv(u_conv, cw, row(conv_b[l]), row(conv_ln_g[l]), row(conv_ln_b[l]),
                       w_conv_pw2[l].astype(BF16), row(b_conv_pw2[l]), row(g_conv_out[l]))
        h = _mix_ffn(o_attn, o_ssm.reshape(seq, bsz * D_SSM), o_conv, h, row(g_attn_out[l]),
                     w_out[l].astype(BF16), row(norm_ffn_g[l]), w_gate_up[l].astype(BF16),
                     w_down[l].astype(BF16), row(final_norm_g), final=(l == depth - 1))
    return h
```

```python
import functools
import math

import jax
import jax.numpy as jnp
from jax import lax
from jax.experimental import pallas as pl
from jax.experimental.pallas import tpu as pltpu

F32 = jnp.float32
BF16 = jnp.bfloat16

ATTN_HEADS = 8
HEAD_DIM = 64
D_ATTN = ATTN_HEADS * HEAD_DIM
SSM_GROUP = 16
SSM_GROUPS = 16
SSM_STATE = 64
D_SSM = SSM_GROUP * SSM_GROUPS
D_CONV = 256
CONV_WIDTH = 31
ROT_DIM = HEAD_DIM // 4
ROPE_THETA = 500000.0
MOBA_BLOCK = 256
MOBA_TOPK = 3
RMS_EPS = 1e-6
LN_EPS = 1e-5

LANES = 128
PAIR = 2 * HEAD_DIM
MASK_NEG = -1e30
SCORE_BOUND = 40.0
KNORM_SLACK = 1.02
BLOCK_GROUP = 4
VMEM_LIMIT = 56 * 1024 * 1024

TOK_TILE = 512
SSM_CHUNK = 64
CONV_ROWS = 64
CONV_HALO = 32


def _rms(x, g):
    return x * lax.rsqrt(jnp.mean(x * x, axis=-1, keepdims=True) + RMS_EPS) * g


def _sigmoid(x):
    return 1.0 / (1.0 + jnp.exp(-x))


def _gelu_tanh(x):
    c = math.sqrt(2.0 / math.pi)
    return 0.5 * x * (1.0 + jnp.tanh(c * (x + 0.044715 * (x * x * x))))


def _params(*sem):
    return pltpu.CompilerParams(dimension_semantics=sem, vmem_limit_bytes=VMEM_LIMIT)


def _inproj_kernel(h_ref, g_ref, w_ref, rc_ref, rs1_ref, rs2_ref,
                   q_ref, k_ref, v_ref, us_ref, uc_ref):
    hn = _rms(h_ref[0], g_ref[...]).astype(BF16)

    def proj(lo, hi):
        return jnp.dot(hn, w_ref[:, lo:hi], preferred_element_type=F32)

    rc, rs1, rs2 = rc_ref[...], rs1_ref[...], rs2_ref[...]

    def rope(t):
        return t * rc + pltpu.roll(t, LANES - ROT_DIM // 2, 1) * rs1 + pltpu.roll(t, ROT_DIM // 2, 1) * rs2

    for c in range(D_ATTN // LANES):
        lo = c * LANES
        q_ref[0, :, lo:lo + LANES] = (rope(proj(lo, lo + LANES)) * (HEAD_DIM ** -0.5)).astype(BF16)
        k_ref[0, :, lo:lo + LANES] = rope(proj(D_ATTN + lo, D_ATTN + lo + LANES)).astype(BF16)
    v_ref[0] = proj(2 * D_ATTN, 3 * D_ATTN).astype(BF16)
    o = 3 * D_ATTN
    us_ref[...] = proj(o, o + D_SSM)
    a = proj(o + D_SSM, o + D_SSM + D_CONV)
    g = proj(o + D_SSM + D_CONV, o + D_SSM + 2 * D_CONV)
    uc_ref[0] = a * _sigmoid(g)


def _inproj(h, g, w, rc, rs1, rs2):
    bsz, seq, d = h.shape
    tm = TOK_TILE
    d_in = w.shape[1]
    tok = lambda n: pl.BlockSpec((1, tm, n), lambda i, b: (b, i, 0))
    full = lambda r, c: pl.BlockSpec((r, c), lambda i, b: (0, 0))
    tab = pl.BlockSpec((tm, LANES), lambda i, b: (i, 0))
    return pl.pallas_call(
        _inproj_kernel,
        grid=(seq // tm, bsz),
        in_specs=[tok(d), full(1, d), full(d, d_in), tab, tab, tab],
        out_specs=[tok(D_ATTN), tok(D_ATTN), tok(D_ATTN),
                   pl.BlockSpec((tm, D_SSM), lambda i, b: (i, b)), tok(D_CONV)],
        out_shape=[jax.ShapeDtypeStruct((bsz, seq, D_ATTN), BF16)] * 3
        + [jax.ShapeDtypeStruct((seq, bsz * D_SSM), F32),
           jax.ShapeDtypeStruct((bsz, seq, D_CONV), F32)],
        compiler_params=_params("arbitrary", "arbitrary"),
        name="inproj",
    )(h, g, w, rc, rs1, rs2)


def _moba_kernel(q_ref, k_ref, v_ref, o_ref, kaug_ref, vt_ref, kmean_ref, knorm_ref, lsum_ref, acc_ref):
    i = pl.program_id(2)
    blk = MOBA_BLOCK
    nb = kaug_ref.shape[0]
    nbp = kmean_ref.shape[0]

    @pl.when(i == 0)
    def _():
        lane = lax.broadcasted_iota(jnp.int32, (blk, LANES), 1)
        kmean_ref[...] = jnp.zeros_like(kmean_ref)
        drow = lax.broadcasted_iota(jnp.int32, (LANES, LANES), 0)
        dcol = lax.broadcasted_iota(jnp.int32, (LANES, LANES), 1)
        head_sum = jnp.where(drow // HEAD_DIM == dcol, 1.0, 0.0).astype(BF16)
        kn2 = jnp.zeros((1, LANES), F32)
        for j in range(nb):
            kj = k_ref[0, j * blk:(j + 1) * blk, :]
            kf = kj.astype(F32)
            kaug_ref[j, :, 0:LANES] = kj
            kaug_ref[j, :, LANES:2 * LANES] = jnp.where(lane == j, 1.0, 0.0).astype(BF16)
            kmean_ref[j:j + 1, :] = jnp.mean(kf, axis=0, keepdims=True)
            vt_ref[j] = v_ref[0, j * blk:(j + 1) * blk, :].astype(F32).T.astype(BF16)
            rown = jnp.dot((kf * kf).astype(BF16), head_sum, preferred_element_type=F32)
            kn2 = jnp.maximum(kn2, jnp.max(rown, axis=0, keepdims=True))
        knorm_ref[...] = jnp.broadcast_to(kn2 * KNORM_SLACK, knorm_ref.shape)

    qt = q_ref[0].astype(F32).T.astype(BF16)
    dim = lax.broadcasted_iota(jnp.int32, (PAIR, blk), 0)
    cand = lax.broadcasted_iota(jnp.int32, (nbp, blk), 0)
    key = lax.broadcasted_iota(jnp.int32, (blk, blk), 0)
    qry = lax.broadcasted_iota(jnp.int32, (blk, blk), 1)
    kmean = kmean_ref[...]
    km_hi = kmean.astype(BF16)
    km_lo = (kmean - km_hi.astype(F32)).astype(BF16)
    k_own = k_ref[0, pl.ds(pl.multiple_of(i * blk, blk), blk), :]
    pad = jnp.zeros((LANES - nbp, blk), BF16)

    qms, q_augs, bound2 = [], [], []
    for head in range(2):
        in_head = (dim < HEAD_DIM) if head == 0 else (dim >= HEAD_DIM)
        qm = jnp.where(in_head, qt, jnp.zeros_like(qt))
        qms.append(qm)
        qn2 = jnp.max(jnp.sum(jnp.square(qm.astype(F32)), axis=0, keepdims=True), axis=1, keepdims=True)
        bound2.append(qn2 * knorm_ref[0:1, head:head + 1])
        gate = (jnp.dot(km_hi, qm, preferred_element_type=F32)
                + jnp.dot(km_lo, qm, preferred_element_type=F32))
        valid = cand < i
        gm = jnp.where(valid, gate, -jnp.inf)
        ahead = jnp.zeros((nbp, blk), jnp.int32)
        for n in range(nb - 1):
            gn = gm[n:n + 1, :]
            beats = (gn > gm) | ((gn == gm) & (cand > n))
            ahead = ahead + jnp.where(beats, 1, 0)
        sel = valid & (ahead < MOBA_TOPK)
        pen = jnp.where(sel, 0.0, MASK_NEG).astype(BF16)
        q_augs.append(jnp.concatenate([qm, pen, pad], axis=0))

    bounded = jnp.max(jnp.maximum(bound2[0], bound2[1])) < SCORE_BOUND * SCORE_BOUND

    def finish(l_a, acc_a, l_b, acc_b):
        inv_a = 1.0 / jnp.sum(l_a, axis=0, keepdims=True)
        inv_b = 1.0 / jnp.sum(l_b, axis=0, keepdims=True)
        o = jnp.where(dim < HEAD_DIM, acc_a * inv_a, acc_b * inv_b)
        o_ref[0] = o.T.astype(o_ref.dtype)

    def pv(j, p):
        return jnp.dot(vt_ref[j], p.astype(BF16), preferred_element_type=F32)

    def colsum8(p):
        return jnp.sum(p.reshape(p.shape[0] // 8, 8, p.shape[1]), axis=0)

    @pl.when(bounded)
    def _():
        def attend(units, carry):
            scores = [jnp.dot(k_own, qms[h], preferred_element_type=F32) if j is None
                      else jnp.dot(kaug_ref[j], q_augs[h], preferred_element_type=F32) for j, h in units]
            probs = [jnp.where(key <= qry, jnp.exp(s), 0.0) if j is None else jnp.exp(s)
                     for (j, h), s in zip(units, scores)]
            out = list(carry)
            for (j, h), p in zip(units, probs):
                out[2 * h] = colsum8(p) if out[2 * h] is None else out[2 * h] + colsum8(p)
            for (j, h), p in zip(units, probs):
                o = pv(i if j is None else j, p)
                out[2 * h + 1] = o if out[2 * h + 1] is None else out[2 * h + 1] + o
            return tuple(out)

        rem = i % BLOCK_GROUP
        for r in range(BLOCK_GROUP):
            @pl.when(rem == r)
            def _():
                units = [(None, 0), (None, 1)] + [(g, h) for g in range(r) for h in range(2)]
                res = attend(units, (None,) * 4)
                for h in range(2):
                    lsum_ref[h] = res[2 * h]
                    acc_ref[h] = res[2 * h + 1]

        def group(g, carry):
            j0 = rem + g * BLOCK_GROUP
            return attend([(j0 + t, h) for t in range(BLOCK_GROUP) for h in range(2)], carry)

        finish(*lax.fori_loop(0, i // BLOCK_GROUP, group, (lsum_ref[0], acc_ref[0], lsum_ref[1], acc_ref[1])))

    @pl.when(jnp.logical_not(bounded))
    def _():
        init = []
        for head in range(2):
            s = jnp.dot(k_own, qms[head], preferred_element_type=F32)
            s = jnp.where(key <= qry, s, -jnp.inf)
            m = jnp.max(s, axis=0, keepdims=True)
            p = jnp.exp(s - m)
            init += [m, jnp.sum(p, axis=0, keepdims=True), pv(i, p)]

        def body(j, carry):
            out = []
            for head in range(2):
                m, l, acc = carry[3 * head:3 * head + 3]
                s = jnp.dot(kaug_ref[j], q_augs[head], preferred_element_type=F32)
                m_new = jnp.maximum(m, jnp.max(s, axis=0, keepdims=True))
                alpha = jnp.exp(m - m_new)
                p = jnp.exp(s - m_new)
                out += [m_new, alpha * l + jnp.sum(p, axis=0, keepdims=True), alpha * acc + pv(j, p)]
            return tuple(out)

        res = lax.fori_loop(0, i, body, tuple(init))
        finish(res[1], res[2], res[4], res[5])


def _moba(q, k, v):
    bsz, seq, _ = q.shape
    blk = MOBA_BLOCK
    nb = seq // blk
    nbp = -(-nb // 16) * 16
    assert nbp <= LANES
    return pl.pallas_call(
        _moba_kernel,
        grid=(bsz, D_ATTN // PAIR, nb),
        in_specs=[pl.BlockSpec((1, blk, PAIR), lambda b, hp, i: (b, i, hp)),
                  pl.BlockSpec((1, seq, PAIR), lambda b, hp, i: (b, 0, hp)),
                  pl.BlockSpec((1, seq, PAIR), lambda b, hp, i: (b, 0, hp))],
        out_specs=pl.BlockSpec((1, blk, PAIR), lambda b, hp, i: (b, i, hp)),
        out_shape=jax.ShapeDtypeStruct((bsz, seq, D_ATTN), BF16),
        scratch_shapes=[pltpu.VMEM((nb, blk, 2 * LANES), BF16), pltpu.VMEM((nb, PAIR, blk), BF16),
                        pltpu.VMEM((nbp, LANES), F32), pltpu.VMEM((8, LANES), F32),
                        pltpu.VMEM((2, 8, blk), F32), pltpu.VMEM((2, PAIR, blk), F32)],
        compiler_params=_params("arbitrary", "arbitrary", "arbitrary"),
        name="moba",
    )(q, k, v)


def _s5_kernel(u_ref, wb_ref, are_ref, aim_ref, wc_ref, d_ref, wg_ref, bg_ref, gn_ref,
               o_ref, xh_ref, st_ref, *, bsz):
    n = SSM_GROUPS * SSM_STATE

    @pl.when(pl.program_id(0) == 0)
    def _():
        st_ref[...] = jnp.zeros_like(st_ref)

    u = u_ref[...]
    xh_ref[...] = jnp.dot(u.astype(BF16), wb_ref[...], preferred_element_type=F32)
    ar, ai = are_ref[...], aim_ref[...]

    def step(t, carry):
        hr, hi = carry
        r0 = pl.multiple_of(t * bsz, bsz)
        xr = xh_ref[pl.ds(r0, bsz), 0:n]
        xi = xh_ref[pl.ds(r0, bsz), n:2 * n]
        nr = ar * hr - ai * hi + xr
        ni = ar * hi + ai * hr + xi
        xh_ref[pl.ds(r0, bsz), 0:n] = nr
        xh_ref[pl.ds(r0, bsz), n:2 * n] = ni
        return nr, ni

    hr, hi = lax.fori_loop(0, u.shape[0] // bsz, step, (st_ref[:, 0:n], st_ref[:, n:2 * n]))
    st_ref[:, 0:n] = hr
    st_ref[:, n:2 * n] = hi

    y = jnp.dot(xh_ref[...].astype(BF16), wc_ref[...], preferred_element_type=F32) + d_ref[...] * u
    z = _gelu_tanh(y)
    gl = jnp.dot(z.astype(BF16), wg_ref[...], preferred_element_type=F32) + bg_ref[...]
    o_ref[...] = _rms(z * _sigmoid(gl), gn_ref[...]).astype(o_ref.dtype)


def _s5(u, wb, a_re, a_im, wc, dsk, wg, bg, gn, bsz):
    rows_total = u.shape[0]
    rows = SSM_CHUNK * bsz
    n2 = 2 * SSM_GROUPS * SSM_STATE
    full = lambda r, c: pl.BlockSpec((r, c), lambda t: (0, 0))
    return pl.pallas_call(
        functools.partial(_s5_kernel, bsz=bsz),
        grid=(rows_total // rows,),
        in_specs=[pl.BlockSpec((rows, D_SSM), lambda t: (t, 0)),
                  full(D_SSM, n2), full(bsz, n2 // 2), full(bsz, n2 // 2), full(n2, D_SSM),
                  full(1, D_SSM), full(D_SSM, D_SSM), full(1, D_SSM), full(1, D_SSM)],
        out_specs=pl.BlockSpec((rows, D_SSM), lambda t: (t, 0)),
        out_shape=jax.ShapeDtypeStruct((rows_total, D_SSM), BF16),
        scratch_shapes=[pltpu.VMEM((rows, n2), F32), pltpu.VMEM((bsz, n2), F32)],
        compiler_params=_params("arbitrary"),
        name="s5",
    )(u, wb, a_re, a_im, wc, dsk, wg, bg, gn)


def _s5_matrices(lam_re, lam_im, log_dt, b_re, b_im, c_re, c_im, bsz):
    dt = jnp.exp(log_dt)[:, None]
    mag = jnp.exp(lam_re * dt)
    ab_re = mag * jnp.cos(lam_im * dt)
    ab_im = mag * jnp.sin(lam_im * dt)
    nr = ab_re - 1.0
    den = lam_re * lam_re + lam_im * lam_im
    f_re = (nr * lam_re + ab_im * lam_im) / den
    f_im = (ab_im * lam_re - nr * lam_im) / den
    bb_re = f_re[..., None] * b_re - f_im[..., None] * b_im
    bb_im = f_re[..., None] * b_im + f_im[..., None] * b_re
    eye = jnp.eye(SSM_GROUPS, dtype=F32)
    n = SSM_GROUPS * SSM_STATE
    wb = lambda bb: jnp.einsum('gpc,gh->gchp', bb, eye).reshape(D_SSM, n)
    wb_all = jnp.concatenate([wb(bb_re), wb(bb_im)], axis=1).astype(BF16)
    wc = lambda cc: jnp.einsum('gcp,gh->gphc', cc, eye).reshape(n, D_SSM)
    wc_all = jnp.concatenate([wc(c_re), -wc(c_im)], axis=0).astype(BF16)
    a_re = jnp.broadcast_to(ab_re.reshape(1, n), (bsz, n))
    a_im = jnp.broadcast_to(ab_im.reshape(1, n), (bsz, n))
    return wb_all, a_re, a_im, wc_all


def _conv_kernel(u_ref, halo_ref, w_ref, cb_ref, lg_ref, lb_ref, pw_ref, pb_ref, gn_ref,
                 o_ref, xx_ref, y_ref):
    tm = u_ref.shape[1]
    first = pl.program_id(1) == 0
    xx_ref[0:CONV_HALO, :] = jnp.where(first, 0.0, halo_ref[0])
    xx_ref[CONV_HALO:, :] = u_ref[0]
    base = CONV_HALO - (CONV_WIDTH - 1)
    for r in range(0, tm, CONV_ROWS):
        acc = jnp.broadcast_to(cb_ref[...], (CONV_ROWS, D_CONV))
        for j in range(CONV_WIDTH):
            acc = acc + w_ref[j:j + 1, :] * xx_ref[r + base + j:r + base + j + CONV_ROWS, :]
        y_ref[r:r + CONV_ROWS, :] = acc
    y = y_ref[...]
    mu = jnp.mean(y, axis=-1, keepdims=True)
    yc = y - mu
    yn = yc * lax.rsqrt(jnp.mean(yc * yc, axis=-1, keepdims=True) + LN_EPS) * lg_ref[...] + lb_ref[...]
    a = yn * _sigmoid(yn)
    out = jnp.dot(a.astype(BF16), pw_ref[...], preferred_element_type=F32) + pb_ref[...]
    o_ref[0] = _rms(out, gn_ref[...]).astype(o_ref.dtype)


def _conv(u, w, cb, lg, lb, pw, pb, gn):
    bsz, seq, _ = u.shape
    tm = TOK_TILE
    per = tm // CONV_HALO
    full = lambda r, c: pl.BlockSpec((r, c), lambda b, i: (0, 0))
    return pl.pallas_call(
        _conv_kernel,
        grid=(bsz, seq // tm),
        in_specs=[pl.BlockSpec((1, tm, D_CONV), lambda b, i: (b, i, 0)),
                  pl.BlockSpec((1, CONV_HALO, D_CONV), lambda b, i: (b, jnp.maximum(i * per - 1, 0), 0)),
                  full(CONV_HALO, D_CONV), full(1, D_CONV), full(1, D_CONV), full(1, D_CONV),
                  full(D_CONV, D_CONV), full(1, D_CONV), full(1, D_CONV)],
        out_specs=pl.BlockSpec((1, tm, D_CONV), lambda b, i: (b, i, 0)),
        out_shape=jax.ShapeDtypeStruct((bsz, seq, D_CONV), BF16),
        scratch_shapes=[pltpu.VMEM((tm + CONV_HALO, D_CONV), F32), pltpu.VMEM((tm, D_CONV), F32)],
        compiler_params=_params("arbitrary", "arbitrary"),
        name="conv",
    )(u, u, w, cb, lg, lb, pw, pb, gn)


def _mix_ffn_kernel(oa_ref, os_ref, oc_ref, h_ref, ga_ref, wo_ref, gf_ref, wgu_ref, wd_ref, gfin_ref,
                    o_ref, act_ref, *, d_ff, ff_chunk, final):
    an = _rms(oa_ref[0].astype(F32), ga_ref[...]).astype(BF16)
    mixed = jnp.concatenate([an, os_ref[...], oc_ref[0]], axis=1)
    h1 = h_ref[0] + jnp.dot(mixed, wo_ref[...], preferred_element_type=F32)
    hn = _rms(h1, gf_ref[...]).astype(BF16)
    for c in range(0, d_ff, ff_chunk):
        gate = jnp.dot(hn, wgu_ref[:, c:c + ff_chunk], preferred_element_type=F32)
        up = jnp.dot(hn, wgu_ref[:, d_ff + c:d_ff + c + ff_chunk], preferred_element_type=F32)
        act_ref[:, c:c + ff_chunk] = (gate * _sigmoid(gate) * up).astype(BF16)
    h2 = h1 + jnp.dot(act_ref[...], wd_ref[...], preferred_element_type=F32)
    if final:
        h2 = _rms(h2, gfin_ref[...])
    o_ref[0] = h2


def _mix_ffn(oa, os2, oc, h, ga, wo, gf, wgu, wd, gfin, final):
    bsz, seq, d = h.shape
    tm = TOK_TILE
    d_ff = wd.shape[0]
    tok = lambda n: pl.BlockSpec((1, tm, n), lambda i, b: (b, i, 0))
    full = lambda r, c: pl.BlockSpec((r, c), lambda i, b: (0, 0), pipeline_mode=pl.Buffered(1))
    return pl.pallas_call(
        functools.partial(_mix_ffn_kernel, d_ff=d_ff, ff_chunk=256, final=final),
        grid=(seq // tm, bsz),
        in_specs=[tok(D_ATTN), pl.BlockSpec((tm, D_SSM), lambda i, b: (i, b)), tok(D_CONV), tok(d),
                  full(1, D_ATTN), full(d, d), full(1, d), full(d, 2 * d_ff), full(d_ff, d), full(1, d)],
        out_specs=tok(d),
        out_shape=jax.ShapeDtypeStruct((bsz, seq, d), F32),
        scratch_shapes=[pltpu.VMEM((tm, d_ff), BF16)],
        compiler_params=_params("arbitrary", "arbitrary"),
        name="mix_ffn",
    )(oa, os2, oc, h, ga, wo, gf, wgu, wd, gfin)


def _rope_tables(seq):
    half = ROT_DIM // 2
    inv = jnp.power(ROPE_THETA, -jnp.arange(0, ROT_DIM, 2, dtype=F32) / ROT_DIM)
    ang = jnp.arange(seq, dtype=F32)[:, None] * inv[None, :]
    cos, sin = jnp.cos(ang), jnp.sin(ang)
    ones = jnp.ones((seq, HEAD_DIM - ROT_DIM), F32)
    zeros = jnp.zeros((seq, HEAD_DIM - ROT_DIM), F32)
    zh = jnp.zeros((seq, half), F32)
    reps = LANES // HEAD_DIM
    rc = jnp.tile(jnp.concatenate([cos, cos, ones], axis=1), (1, reps))
    rs1 = jnp.tile(jnp.concatenate([-sin, zh, zeros], axis=1), (1, reps))
    rs2 = jnp.tile(jnp.concatenate([zh, sin, zeros], axis=1), (1, reps))
    return rc, rs1, rs2


def kernel(x, norm_mix_g, w_in, lambda_re, lambda_im, log_dt, b_re, b_im, c_re, c_im, d_skip, w_ssm_glu, b_ssm_glu, conv_w, conv_b, conv_ln_g, conv_ln_b, w_conv_pw2, b_conv_pw2, g_attn_out, g_ssm_out, g_conv_out, w_out, norm_ffn_g, w_gate_up, w_down, final_norm_g):
    bsz, seq, d = x.shape
    depth = w_in.shape[0]
    rc, rs1, rs2 = _rope_tables(seq)
    row = lambda a: a.reshape(1, -1)
    h = x
    for l in range(depth):
        q, k, v, u_ssm, u_conv = _inproj(h, row(norm_mix_g[l]), w_in[l].astype(BF16), rc, rs1, rs2)
        o_attn = _moba(q, k, v)
        wb, a_re, a_im, wc = _s5_matrices(lambda_re[l], lambda_im[l], log_dt[l], b_re[l], b_im[l],
                                          c_re[l], c_im[l], bsz)
        o_ssm = _s5(u_ssm.reshape(seq * bsz, D_SSM), wb, a_re, a_im, wc, row(d_skip[l]),
                    w_ssm_glu[l].astype(BF16), row(b_ssm_glu[l]), row(g_ssm_out[l]), bsz)
        cw = jnp.pad(conv_w[l], ((0, CONV_HALO - CONV_WIDTH), (0, 0)))
        o_conv = _conv(u_conv, cw, row(conv_b[l]), row(conv_ln_g[l]), row(conv_ln_b[l]),
                       w_conv_pw2[l].astype(BF16), row(b_conv_pw2[l]), row(g_conv_out[l]))
        h = _mix_ffn(o_attn, o_ssm.reshape(seq, bsz * D_SSM), o_conv, h, row(g_attn_out[l]),
                     w_out[l].astype(BF16), row(norm_ffn_g[l]), w_gate_up[l].astype(BF16),
                     w_down[l].astype(BF16), row(final_norm_g), final=(l == depth - 1))
    return h
```

```python
import functools
import math

import jax
import jax.numpy as jnp
from jax import lax
from jax.experimental import pallas as pl
from jax.experimental.pallas import tpu as pltpu

F32 = jnp.float32
BF16 = jnp.bfloat16

ATTN_HEADS = 8
HEAD_DIM = 64
D_ATTN = ATTN_HEADS * HEAD_DIM
SSM_GROUP = 16
SSM_GROUPS = 16
SSM_STATE = 64
D_SSM = SSM_GROUP * SSM_GROUPS
D_CONV = 256
CONV_WIDTH = 31
ROT_DIM = HEAD_DIM // 4
ROPE_THETA = 500000.0
MOBA_BLOCK = 256
MOBA_TOPK = 3
RMS_EPS = 1e-6
LN_EPS = 1e-5

LANES = 128
SUBLANES = 8
PAIR = 2 * HEAD_DIM
MASK_NEG = -1e30
SCORE_BOUND = 40.0
KNORM_SLACK = 1.02
BLOCK_GROUP = 8
VMEM_LIMIT = 56 * 1024 * 1024

TOK_TILE = 512
SSM_CHUNK = 64
CONV_ROWS = 64
CONV_HALO = 32


def _rms(x, g):
    return x * lax.rsqrt(jnp.mean(x * x, axis=-1, keepdims=True) + RMS_EPS) * g


def _sigmoid(x):
    return 1.0 / (1.0 + jnp.exp(-x))


def _gelu_tanh(x):
    c = math.sqrt(2.0 / math.pi)
    return 0.5 * x * (1.0 + jnp.tanh(c * (x + 0.044715 * (x * x * x))))


def _params(*sem):
    return pltpu.CompilerParams(dimension_semantics=sem, vmem_limit_bytes=VMEM_LIMIT)


def _inproj_kernel(h_ref, g_ref, w_ref, rc_ref, rs1_ref, rs2_ref,
                   q_ref, k_ref, v_ref, us_ref, uc_ref):
    hn = _rms(h_ref[0], g_ref[...]).astype(BF16)

    def proj(lo, hi):
        return jnp.dot(hn, w_ref[:, lo:hi], preferred_element_type=F32)

    rc, rs1, rs2 = rc_ref[...], rs1_ref[...], rs2_ref[...]

    def rope(t):
        return t * rc + pltpu.roll(t, LANES - ROT_DIM // 2, 1) * rs1 + pltpu.roll(t, ROT_DIM // 2, 1) * rs2

    q = proj(0, D_ATTN)
    k = proj(D_ATTN, 2 * D_ATTN)
    for c in range(D_ATTN // LANES):
        lo = c * LANES
        q_ref[0, :, lo:lo + LANES] = (rope(q[:, lo:lo + LANES]) * (HEAD_DIM ** -0.5)).astype(BF16)
        k_ref[0, :, lo:lo + LANES] = rope(k[:, lo:lo + LANES]).astype(BF16)
    v_ref[0] = proj(2 * D_ATTN, 3 * D_ATTN).astype(BF16)
    o = 3 * D_ATTN
    us_ref[0] = proj(o, o + D_SSM)
    a = proj(o + D_SSM, o + D_SSM + D_CONV)
    g = proj(o + D_SSM + D_CONV, o + D_SSM + 2 * D_CONV)
    uc_ref[0] = a * _sigmoid(g)


def _inproj(h, g, w, rc, rs1, rs2):
    bsz, seq, d = h.shape
    tm = TOK_TILE
    d_in = w.shape[1]
    tok = lambda n: pl.BlockSpec((1, tm, n), lambda i, b: (b, i, 0))
    full = lambda r, c: pl.BlockSpec((r, c), lambda i, b: (0, 0))
    tab = pl.BlockSpec((tm, LANES), lambda i, b: (i, 0))
    return pl.pallas_call(
        _inproj_kernel,
        grid=(seq // tm, bsz),
        in_specs=[tok(d), full(1, d), full(d, d_in), tab, tab, tab],
        out_specs=[tok(D_ATTN), tok(D_ATTN), tok(D_ATTN), tok(D_SSM), tok(D_CONV)],
        out_shape=[jax.ShapeDtypeStruct((bsz, seq, D_ATTN), BF16)] * 3
        + [jax.ShapeDtypeStruct((bsz, seq, D_SSM), F32),
           jax.ShapeDtypeStruct((bsz, seq, D_CONV), F32)],
        compiler_params=_params("arbitrary", "arbitrary"),
        name="inproj",
    )(h, g, w, rc, rs1, rs2)


def _moba_kernel(q_ref, k_ref, v_ref, o_ref, kaug_ref, vt_ref, kmean_ref, knorm_ref, lsum_ref, acc_ref):
    i = pl.program_id(2)
    blk = MOBA_BLOCK
    nb = kaug_ref.shape[0]
    nbp = kmean_ref.shape[0]

    @pl.when(i == 0)
    def _():
        lane = lax.broadcasted_iota(jnp.int32, (blk, LANES), 1)
        kmean_ref[...] = jnp.zeros_like(kmean_ref)
        drow = lax.broadcasted_iota(jnp.int32, (LANES, LANES), 0)
        dcol = lax.broadcasted_iota(jnp.int32, (LANES, LANES), 1)
        head_sum = jnp.where(drow // HEAD_DIM == dcol, 1.0, 0.0).astype(BF16)
        kn2 = jnp.zeros((1, LANES), F32)
        for j in range(nb):
            kj = k_ref[0, j * blk:(j + 1) * blk, :]
            kf = kj.astype(F32)
            kaug_ref[j, :, 0:LANES] = kj
            kaug_ref[j, :, LANES:2 * LANES] = jnp.where(lane == j, 1.0, 0.0).astype(BF16)
            kmean_ref[j:j + 1, :] = jnp.mean(kf, axis=0, keepdims=True)
            vt_ref[j] = v_ref[0, j * blk:(j + 1) * blk, :].astype(F32).T.astype(BF16)
            rown = jnp.dot((kf * kf).astype(BF16), head_sum, preferred_element_type=F32)
            kn2 = jnp.maximum(kn2, jnp.max(rown, axis=0, keepdims=True))
        knorm_ref[...] = jnp.broadcast_to(kn2 * KNORM_SLACK, knorm_ref.shape)

    qt = q_ref[0].astype(F32).T.astype(BF16)
    dim = lax.broadcasted_iota(jnp.int32, (PAIR, blk), 0)
    cand = lax.broadcasted_iota(jnp.int32, (nbp, blk), 0)
    key = lax.broadcasted_iota(jnp.int32, (blk, blk), 0)
    qry = lax.broadcasted_iota(jnp.int32, (blk, blk), 1)
    kmean = kmean_ref[...]
    km_hi = kmean.astype(BF16)
    km_lo = (kmean - km_hi.astype(F32)).astype(BF16)
    k_own = k_ref[0, pl.ds(pl.multiple_of(i * blk, blk), blk), :]
    pad = jnp.zeros((LANES - nbp, blk), BF16)

    qms, q_augs, bound2 = [], [], []
    for head in range(2):
        in_head = (dim < HEAD_DIM) if head == 0 else (dim >= HEAD_DIM)
        qm = jnp.where(in_head, qt, jnp.zeros_like(qt))
        qms.append(qm)
        qn2 = jnp.max(jnp.sum(jnp.square(qm.astype(F32)), axis=0, keepdims=True), axis=1, keepdims=True)
        bound2.append(qn2 * knorm_ref[0:1, head:head + 1])
        gate = (jnp.dot(km_hi, qm, preferred_element_type=F32)
                + jnp.dot(km_lo, qm, preferred_element_type=F32))
        valid = cand < i
        gm = jnp.where(valid, gate, -jnp.inf)
        ahead = jnp.zeros((nbp, blk), jnp.int32)
        for n in range(nb - 1):
            gn = gm[n:n + 1, :]
            beats = (gn > gm) | ((gn == gm) & (cand > n))
            ahead = ahead + jnp.where(beats, 1, 0)
        sel = valid & (ahead < MOBA_TOPK)
        pen = jnp.where(sel, 0.0, MASK_NEG).astype(BF16)
        q_augs.append(jnp.concatenate([qm, pen, pad], axis=0))

    bounded = jnp.max(jnp.maximum(bound2[0], bound2[1])) < SCORE_BOUND * SCORE_BOUND

    def finish(l_a, acc_a, l_b, acc_b):
        inv_a = 1.0 / jnp.sum(l_a, axis=0, keepdims=True)
        inv_b = 1.0 / jnp.sum(l_b, axis=0, keepdims=True)
        o = jnp.concatenate([acc_a * inv_a, acc_b * inv_b], axis=0)
        o_ref[0] = o.T.astype(o_ref.dtype)

    def pv(j, h, p):
        return jnp.dot(vt_ref[j, h * HEAD_DIM:(h + 1) * HEAD_DIM, :], p.astype(BF16),
                       preferred_element_type=F32)

    def colsum8(p):
        return jnp.sum(p.reshape(p.shape[0] // SUBLANES, SUBLANES, p.shape[1]), axis=0)

    @pl.when(bounded)
    def _():
        def attend(units, carry):
            scores = [jnp.dot(k_own, qms[h], preferred_element_type=F32) if j is None
                      else jnp.dot(kaug_ref[j], q_augs[h], preferred_element_type=F32) for j, h in units]
            probs = [jnp.where(key <= qry, jnp.exp(s), 0.0) if j is None else jnp.exp(s)
                     for (j, h), s in zip(units, scores)]
            out = list(carry)
            for (j, h), p in zip(units, probs):
                out[2 * h] = colsum8(p) if out[2 * h] is None else out[2 * h] + colsum8(p)
            for (j, h), p in zip(units, probs):
                o = pv(i if j is None else j, h, p)
                out[2 * h + 1] = o if out[2 * h + 1] is None else out[2 * h + 1] + o
            return tuple(out)

        rem = i % BLOCK_GROUP
        for r in range(BLOCK_GROUP):
            @pl.when(rem == r)
            def _():
                units = [(None, 0), (None, 1)] + [(g, h) for g in range(r) for h in range(2)]
                res = attend(units, (None,) * 4)
                for h in range(2):
                    lsum_ref[h] = res[2 * h]
                    acc_ref[h] = res[2 * h + 1]

        def group(g, carry):
            j0 = rem + g * BLOCK_GROUP
            return attend([(j0 + t, h) for t in range(BLOCK_GROUP) for h in range(2)], carry)

        finish(*lax.fori_loop(0, i // BLOCK_GROUP, group, (lsum_ref[0], acc_ref[0], lsum_ref[1], acc_ref[1])))

    @pl.when(jnp.logical_not(bounded))
    def _():
        init = []
        for head in range(2):
            s = jnp.dot(k_own, qms[head], preferred_element_type=F32)
            s = jnp.where(key <= qry, s, -jnp.inf)
            m = jnp.max(s, axis=0, keepdims=True)
            p = jnp.exp(s - m)
            init += [m, jnp.sum(p, axis=0, keepdims=True), pv(i, head, p)]

        def body(j, carry):
            out = []
            for head in range(2):
                m, l, acc = carry[3 * head:3 * head + 3]
                s = jnp.dot(kaug_ref[j], q_augs[head], preferred_element_type=F32)
                m_new = jnp.maximum(m, jnp.max(s, axis=0, keepdims=True))
                alpha = jnp.exp(m - m_new)
                p = jnp.exp(s - m_new)
                out += [m_new, alpha * l + jnp.sum(p, axis=0, keepdims=True), alpha * acc + pv(j, head, p)]
            return tuple(out)

        res = lax.fori_loop(0, i, body, tuple(init))
        finish(res[1], res[2], res[4], res[5])


def _moba(q, k, v):
    bsz, seq, _ = q.shape
    blk = MOBA_BLOCK
    nb = seq // blk
    nbp = -(-nb // 16) * 16
    assert nbp <= LANES
    return pl.pallas_call(
        _moba_kernel,
        grid=(bsz, D_ATTN // PAIR, nb),
        in_specs=[pl.BlockSpec((1, blk, PAIR), lambda b, hp, i: (b, i, hp)),
                  pl.BlockSpec((1, seq, PAIR), lambda b, hp, i: (b, 0, hp)),
                  pl.BlockSpec((1, seq, PAIR), lambda b, hp, i: (b, 0, hp))],
        out_specs=pl.BlockSpec((1, blk, PAIR), lambda b, hp, i: (b, i, hp)),
        out_shape=jax.ShapeDtypeStruct((bsz, seq, D_ATTN), BF16),
        scratch_shapes=[pltpu.VMEM((nb, blk, 2 * LANES), BF16), pltpu.VMEM((nb, PAIR, blk), BF16),
                        pltpu.VMEM((nbp, LANES), F32), pltpu.VMEM((SUBLANES, LANES), F32),
                        pltpu.VMEM((2, SUBLANES, blk), F32), pltpu.VMEM((2, HEAD_DIM, blk), F32)],
        compiler_params=_params("arbitrary", "arbitrary", "arbitrary"),
        name="moba",
    )(q, k, v)


def _s5_kernel(u_ref, wb_ref, are_ref, aim_ref, wc_ref, d_ref, wg_ref, bg_ref, gn_ref,
               o_ref, xh_ref, st_ref, tb_ref):
    n = SSM_GROUPS * SSM_STATE
    bsz, steps, _ = u_ref.shape
    halves = D_SSM // LANES

    @pl.when(pl.program_id(0) == 0)
    def _():
        st_ref[...] = jnp.zeros_like(st_ref)

    for b in range(bsz):
        for c in range(halves):
            tb_ref[c, pl.ds(b, steps, stride=bsz), :] = u_ref[b, :, c * LANES:(c + 1) * LANES]
    u = jnp.concatenate([tb_ref[c] for c in range(halves)], axis=1)
    xh_ref[...] = jnp.dot(u.astype(BF16), wb_ref[...], preferred_element_type=F32)
    ar, ai = are_ref[...], aim_ref[...]

    def step(t, carry):
        hr, hi = carry
        r0 = pl.multiple_of(t * bsz, bsz)
        xr = xh_ref[pl.ds(r0, bsz), 0:n]
        xi = xh_ref[pl.ds(r0, bsz), n:2 * n]
        nr = ar * hr - ai * hi + xr
        ni = ar * hi + ai * hr + xi
        xh_ref[pl.ds(r0, bsz), 0:n] = nr
        xh_ref[pl.ds(r0, bsz), n:2 * n] = ni
        return nr, ni

    hr, hi = lax.fori_loop(0, steps, step, (st_ref[:, 0:n], st_ref[:, n:2 * n]))
    st_ref[:, 0:n] = hr
    st_ref[:, n:2 * n] = hi

    y = jnp.dot(xh_ref[...].astype(BF16), wc_ref[...], preferred_element_type=F32) + d_ref[...] * u
    z = _gelu_tanh(y)
    gl = jnp.dot(z.astype(BF16), wg_ref[...], preferred_element_type=F32) + bg_ref[...]
    out = _rms(z * _sigmoid(gl), gn_ref[...])
    for c in range(halves):
        tb_ref[c] = out[:, c * LANES:(c + 1) * LANES]
    for b in range(bsz):
        for c in range(halves):
            o_ref[b, :, c * LANES:(c + 1) * LANES] = tb_ref[c, pl.ds(b, steps, stride=bsz), :].astype(o_ref.dtype)


def _s5(u, wb, a_re, a_im, wc, dsk, wg, bg, gn):
    bsz, seq, _ = u.shape
    rows = SSM_CHUNK * bsz
    n2 = 2 * SSM_GROUPS * SSM_STATE
    full = lambda r, c: pl.BlockSpec((r, c), lambda t: (0, 0))
    return pl.pallas_call(
        _s5_kernel,
        grid=(seq // SSM_CHUNK,),
        in_specs=[pl.BlockSpec((bsz, SSM_CHUNK, D_SSM), lambda t: (0, t, 0)),
                  full(D_SSM, n2), full(bsz, n2 // 2), full(bsz, n2 // 2), full(n2, D_SSM),
                  full(1, D_SSM), full(D_SSM, D_SSM), full(1, D_SSM), full(1, D_SSM)],
        out_specs=pl.BlockSpec((bsz, SSM_CHUNK, D_SSM), lambda t: (0, t, 0)),
        out_shape=jax.ShapeDtypeStruct((bsz, seq, D_SSM), BF16),
        scratch_shapes=[pltpu.VMEM((rows, n2), F32), pltpu.VMEM((bsz, n2), F32),
                        pltpu.VMEM((D_SSM // LANES, rows, LANES), F32)],
        compiler_params=_params("arbitrary"),
        name="s5",
    )(u, wb, a_re, a_im, wc, dsk, wg, bg, gn)


def _s5_matrices(lam_re, lam_im, log_dt, b_re, b_im, c_re, c_im, bsz):
    dt = jnp.exp(log_dt)[:, None]
    mag = jnp.exp(lam_re * dt)
    ab_re = mag * jnp.cos(lam_im * dt)
    ab_im = mag * jnp.sin(lam_im * dt)
    nr = ab_re - 1.0
    den = lam_re * lam_re + lam_im * lam_im
    f_re = (nr * lam_re + ab_im * lam_im) / den
    f_im = (ab_im * lam_re - nr * lam_im) / den
    bb_re = f_re[..., None] * b_re - f_im[..., None] * b_im
    bb_im = f_re[..., None] * b_im + f_im[..., None] * b_re
    eye = jnp.eye(SSM_GROUPS, dtype=F32)
    n = SSM_GROUPS * SSM_STATE
    wb = lambda bb: jnp.einsum('gpc,gh->gchp', bb, eye).reshape(D_SSM, n)
    wb_all = jnp.concatenate([wb(bb_re), wb(bb_im)], axis=1).astype(BF16)
    wc = lambda cc: jnp.einsum('gcp,gh->gphc', cc, eye).reshape(n, D_SSM)
    wc_all = jnp.concatenate([wc(c_re), -wc(c_im)], axis=0).astype(BF16)
    a_re = jnp.broadcast_to(ab_re.reshape(1, n), (bsz, n))
    a_im = jnp.broadcast_to(ab_im.reshape(1, n), (bsz, n))
    return wb_all, a_re, a_im, wc_all


def _conv_kernel(u_ref, halo_ref, w_ref, cb_ref, lg_ref, lb_ref, pw_ref, pb_ref, gn_ref,
                 o_ref, xx_ref, y_ref):
    tm = u_ref.shape[1]
    first = pl.program_id(1) == 0
    xx_ref[0:CONV_HALO, :] = jnp.where(first, 0.0, halo_ref[0])
    xx_ref[CONV_HALO:, :] = u_ref[0]
    base = CONV_HALO - (CONV_WIDTH - 1)
    for r in range(0, tm, CONV_ROWS):
        acc = jnp.broadcast_to(cb_ref[...], (CONV_ROWS, D_CONV))
        for s in range(SUBLANES):
            rows = CONV_ROWS + (SUBLANES if s else 0)
            z = None
            for o in range(s, CONV_HALO + 1, SUBLANES):
                j = o - base
                if 0 <= j < CONV_WIDTH:
                    term = w_ref[j:j + 1, :] * xx_ref[r + o - s:r + o - s + rows, :]
                    z = term if z is None else z + term
            acc = acc + z[s:s + CONV_ROWS]
        y_ref[r:r + CONV_ROWS, :] = acc
    y = y_ref[...]
    mu = jnp.mean(y, axis=-1, keepdims=True)
    yc = y - mu
    yn = yc * lax.rsqrt(jnp.mean(yc * yc, axis=-1, keepdims=True) + LN_EPS) * lg_ref[...] + lb_ref[...]
    a = yn * _sigmoid(yn)
    out = jnp.dot(a.astype(BF16), pw_ref[...], preferred_element_type=F32) + pb_ref[...]
    o_ref[0] = _rms(out, gn_ref[...]).astype(o_ref.dtype)


def _conv(u, w, cb, lg, lb, pw, pb, gn):
    bsz, seq, _ = u.shape
    tm = TOK_TILE
    per = tm // CONV_HALO
    full = lambda r, c: pl.BlockSpec((r, c), lambda b, i: (0, 0))
    return pl.pallas_call(
        _conv_kernel,
        grid=(bsz, seq // tm),
        in_specs=[pl.BlockSpec((1, tm, D_CONV), lambda b, i: (b, i, 0)),
                  pl.BlockSpec((1, CONV_HALO, D_CONV), lambda b, i: (b, jnp.maximum(i * per - 1, 0), 0)),
                  full(CONV_HALO, D_CONV), full(1, D_CONV), full(1, D_CONV), full(1, D_CONV),
                  full(D_CONV, D_CONV), full(1, D_CONV), full(1, D_CONV)],
        out_specs=pl.BlockSpec((1, tm, D_CONV), lambda b, i: (b, i, 0)),
        out_shape=jax.ShapeDtypeStruct((bsz, seq, D_CONV), BF16),
        scratch_shapes=[pltpu.VMEM((tm + CONV_HALO, D_CONV), F32), pltpu.VMEM((tm, D_CONV), F32)],
        compiler_params=_params("arbitrary", "arbitrary"),
        name="conv",
    )(u, u, w, cb, lg, lb, pw, pb, gn)


def _mix_ffn_kernel(oa_ref, os_ref, oc_ref, h_ref, ga_ref, wo_ref, gf_ref, wgu_ref, wd_ref, gfin_ref,
                    o_ref, act_ref, *, d_ff, ff_chunk, final):
    an = _rms(oa_ref[0].astype(F32), ga_ref[...]).astype(BF16)
    mixed = jnp.concatenate([an, os_ref[0], oc_ref[0]], axis=1)
    h1 = h_ref[0] + jnp.dot(mixed, wo_ref[...], preferred_element_type=F32)
    hn = _rms(h1, gf_ref[...]).astype(BF16)
    for c in range(0, d_ff, ff_chunk):
        gate = jnp.dot(hn, wgu_ref[:, c:c + ff_chunk], preferred_element_type=F32)
        up = jnp.dot(hn, wgu_ref[:, d_ff + c:d_ff + c + ff_chunk], preferred_element_type=F32)
        act_ref[:, c:c + ff_chunk] = (gate * _sigmoid(gate) * up).astype(BF16)
    h2 = h1 + jnp.dot(act_ref[...], wd_ref[...], preferred_element_type=F32)
    if final:
        h2 = _rms(h2, gfin_ref[...])
    o_ref[0] = h2


def _mix_ffn(oa, os2, oc, h, ga, wo, gf, wgu, wd, gfin, final):
    bsz, seq, d = h.shape
    tm = TOK_TILE
    d_ff = wd.shape[0]
    tok = lambda n: pl.BlockSpec((1, tm, n), lambda i, b: (b, i, 0))
    full = lambda r, c: pl.BlockSpec((r, c), lambda i, b: (0, 0), pipeline_mode=pl.Buffered(1))
    return pl.pallas_call(
        functools.partial(_mix_ffn_kernel, d_ff=d_ff, ff_chunk=256, final=final),
        grid=(seq // tm, bsz),
        in_specs=[tok(D_ATTN), tok(D_SSM), tok(D_CONV), tok(d),
                  full(1, D_ATTN), full(d, d), full(1, d), full(d, 2 * d_ff), full(d_ff, d), full(1, d)],
        out_specs=tok(d),
        out_shape=jax.ShapeDtypeStruct((bsz, seq, d), F32),
        scratch_shapes=[pltpu.VMEM((tm, d_ff), BF16)],
        compiler_params=_params("arbitrary", "arbitrary"),
        name="mix_ffn",
    )(oa, os2, oc, h, ga, wo, gf, wgu, wd, gfin)


def _rope_tables(seq):
    half = ROT_DIM // 2
    inv = jnp.power(ROPE_THETA, -jnp.arange(0, ROT_DIM, 2, dtype=F32) / ROT_DIM)
    ang = jnp.arange(seq, dtype=F32)[:, None] * inv[None, :]
    cos, sin = jnp.cos(ang), jnp.sin(ang)
    ones = jnp.ones((seq, HEAD_DIM - ROT_DIM), F32)
    zeros = jnp.zeros((seq, HEAD_DIM - ROT_DIM), F32)
    zh = jnp.zeros((seq, half), F32)
    reps = LANES // HEAD_DIM
    rc = jnp.tile(jnp.concatenate([cos, cos, ones], axis=1), (1, reps))
    rs1 = jnp.tile(jnp.concatenate([-sin, zh, zeros], axis=1), (1, reps))
    rs2 = jnp.tile(jnp.concatenate([zh, sin, zeros], axis=1), (1, reps))
    return rc, rs1, rs2


def kernel(x, norm_mix_g, w_in, lambda_re, lambda_im, log_dt, b_re, b_im, c_re, c_im, d_skip, w_ssm_glu, b_ssm_glu, conv_w, conv_b, conv_ln_g, conv_ln_b, w_conv_pw2, b_conv_pw2, g_attn_out, g_ssm_out, g_conv_out, w_out, norm_ffn_g, w_gate_up, w_down, final_norm_g):
    bsz, seq, d = x.shape
    depth = w_in.shape[0]
    rc, rs1, rs2 = _rope_tables(seq)
    row = lambda a: a.reshape(1, -1)
    h = x
    for l in range(depth):
        q, k, v, u_ssm, u_conv = _inproj(h, row(norm_mix_g[l]), w_in[l].astype(BF16), rc, rs1, rs2)
        o_attn = _moba(q, k, v)
        wb, a_re, a_im, wc = _s5_matrices(lambda_re[l], lambda_im[l], log_dt[l], b_re[l], b_im[l],
                                          c_re[l], c_im[l], bsz)
        o_ssm = _s5(u_ssm, wb, a_re, a_im, wc, row(d_skip[l]),
                    w_ssm_glu[l].astype(BF16), row(b_ssm_glu[l]), row(g_ssm_out[l]))
        cw = jnp.pad(conv_w[l], ((0, CONV_HALO - CONV_WIDTH), (0, 0)))
        o_conv = _conv(u_conv, cw, row(conv_b[l]), row(conv_ln_g[l]), row(conv_ln_b[l]),
                       w_conv_pw2[l].astype(BF16), row(b_conv_pw2[l]), row(g_conv_out[l]))
        h = _mix_ffn(o_attn, o_ssm, o_conv, h, row(g_attn_out[l]),
                     w_out[l].astype(BF16), row(norm_ffn_g[l]), w_gate_up[l].astype(BF16),
                     w_down[l].astype(BF16), row(final_norm_g), final=(l == depth - 1))
    return h
```

```python
import functools
import math

import jax
import jax.numpy as jnp
from jax import lax
from jax.experimental import pallas as pl
from jax.experimental.pallas import tpu as pltpu

F32 = jnp.float32
BF16 = jnp.bfloat16

ATTN_HEADS = 8
HEAD_DIM = 64
D_ATTN = ATTN_HEADS * HEAD_DIM
SSM_GROUP = 16
SSM_GROUPS = 16
SSM_STATE = 64
D_SSM = SSM_GROUP * SSM_GROUPS
D_CONV = 256
CONV_WIDTH = 31
ROT_DIM = HEAD_DIM // 4
ROPE_THETA = 500000.0
MOBA_BLOCK = 256
MOBA_TOPK = 3
RMS_EPS = 1e-6
LN_EPS = 1e-5

LANES = 128
SUBLANES = 8
PAIR = 2 * HEAD_DIM
MASK_NEG = -1e30
SCORE_BOUND = 40.0
NORM_SLACK = 1.02
BLOCK_GROUP = 8
VMEM_LIMIT = 56 * 1024 * 1024

TOK_TILE = 512
SSM_CHUNK = 64
CONV_ROWS = 64
CONV_HALO = 32


def _rms(x, g):
    return x * lax.rsqrt(jnp.mean(x * x, axis=-1, keepdims=True) + RMS_EPS) * g


def _sigmoid(x):
    return 1.0 / (1.0 + jnp.exp(-x))


def _gelu_tanh(x):
    c = math.sqrt(2.0 / math.pi)
    return 0.5 * x * (1.0 + jnp.tanh(c * (x + 0.044715 * (x * x * x))))


def _params(*sem):
    return pltpu.CompilerParams(dimension_semantics=sem, vmem_limit_bytes=VMEM_LIMIT)


def _inproj_kernel(h_ref, g_ref, w_ref, rc_ref, rs1_ref, rs2_ref,
                   cw_ref, cb_ref, lg_ref, lb_ref, pw_ref, pb_ref, gn_ref,
                   q_ref, k_ref, v_ref, us_ref, oc_ref, xx_ref, y_ref):
    tm = h_ref.shape[1]
    hn = _rms(h_ref[0], g_ref[...]).astype(BF16)

    def proj(lo, hi):
        return jnp.dot(hn, w_ref[:, lo:hi], preferred_element_type=F32)

    first = pl.program_id(1) == 0

    @pl.when(first)
    def _():
        xx_ref[0:CONV_HALO, :] = jnp.zeros((CONV_HALO, D_CONV), F32)

    @pl.when(jnp.logical_not(first))
    def _():
        xx_ref[0:CONV_HALO, :] = xx_ref[tm:tm + CONV_HALO, :]

    o = 3 * D_ATTN
    a = proj(o + D_SSM, o + D_SSM + D_CONV)
    g = proj(o + D_SSM + D_CONV, o + D_SSM + 2 * D_CONV)
    xx_ref[CONV_HALO:, :] = a * _sigmoid(g)
    rc, rs1, rs2 = rc_ref[...], rs1_ref[...], rs2_ref[...]

    def rope(t):
        return t * rc + pltpu.roll(t, LANES - ROT_DIM // 2, 1) * rs1 + pltpu.roll(t, ROT_DIM // 2, 1) * rs2

    q = proj(0, D_ATTN)
    k = proj(D_ATTN, 2 * D_ATTN)
    for c in range(D_ATTN // LANES):
        lo = c * LANES
        q_ref[0, :, lo:lo + LANES] = (rope(q[:, lo:lo + LANES]) * (HEAD_DIM ** -0.5)).astype(BF16)
        k_ref[0, :, lo:lo + LANES] = rope(k[:, lo:lo + LANES]).astype(BF16)
    v_ref[0] = proj(2 * D_ATTN, 3 * D_ATTN).astype(BF16)
    us_ref[0] = proj(o, o + D_SSM)

    base = CONV_HALO - (CONV_WIDTH - 1)
    for r in range(0, tm, CONV_ROWS):
        acc = jnp.broadcast_to(cb_ref[...], (CONV_ROWS, D_CONV))
        for s in range(SUBLANES):
            rows = CONV_ROWS + (SUBLANES if s else 0)
            z = None
            for off in range(s, CONV_HALO + 1, SUBLANES):
                j = off - base
                if 0 <= j < CONV_WIDTH:
                    term = cw_ref[j:j + 1, :] * xx_ref[r + off - s:r + off - s + rows, :]
                    z = term if z is None else z + term
            acc = acc + z[s:s + CONV_ROWS]
        y_ref[r:r + CONV_ROWS, :] = acc

    y = y_ref[...]
    mu = jnp.mean(y, axis=-1, keepdims=True)
    yc = y - mu
    yn = yc * lax.rsqrt(jnp.mean(yc * yc, axis=-1, keepdims=True) + LN_EPS) * lg_ref[...] + lb_ref[...]
    act = yn * _sigmoid(yn)
    out = jnp.dot(act.astype(BF16), pw_ref[...], preferred_element_type=F32) + pb_ref[...]
    oc_ref[0] = _rms(out, gn_ref[...]).astype(oc_ref.dtype)


def _inproj(h, g, w, rc, rs1, rs2, cw, cb, lg, lb, pw, pb, gn):
    bsz, seq, d = h.shape
    tm = TOK_TILE
    d_in = w.shape[1]
    tok = lambda n: pl.BlockSpec((1, tm, n), lambda b, i: (b, i, 0))
    full = lambda r, c: pl.BlockSpec((r, c), lambda b, i: (0, 0))
    tab = pl.BlockSpec((tm, LANES), lambda b, i: (i, 0))
    return pl.pallas_call(
        _inproj_kernel,
        grid=(bsz, seq // tm),
        in_specs=[tok(d), full(1, d), full(d, d_in), tab, tab, tab,
                  full(CONV_HALO, D_CONV), full(1, D_CONV), full(1, D_CONV), full(1, D_CONV),
                  full(D_CONV, D_CONV), full(1, D_CONV), full(1, D_CONV)],
        out_specs=[tok(D_ATTN), tok(D_ATTN), tok(D_ATTN), tok(D_SSM), tok(D_CONV)],
        out_shape=[jax.ShapeDtypeStruct((bsz, seq, D_ATTN), BF16)] * 3
        + [jax.ShapeDtypeStruct((bsz, seq, D_SSM), F32),
           jax.ShapeDtypeStruct((bsz, seq, D_CONV), BF16)],
        scratch_shapes=[pltpu.VMEM((tm + CONV_HALO, D_CONV), F32), pltpu.VMEM((tm, D_CONV), F32)],
        compiler_params=_params("arbitrary", "arbitrary"),
        name="inproj",
    )(h, g, w, rc, rs1, rs2, cw, cb, lg, lb, pw, pb, gn)


def _moba_kernel(q_ref, k_ref, v_ref, o_ref, kaug_ref, vt_ref, kmean_ref, qaug_ref, lsum_ref, acc_ref,
                 flag_ref):
    i = pl.program_id(2)
    blk = MOBA_BLOCK
    nb = kaug_ref.shape[0]
    nbp = kmean_ref.shape[0]

    @pl.when(i == 0)
    def _():
        lane = lax.broadcasted_iota(jnp.int32, (blk, LANES), 1)
        kmean_ref[...] = jnp.zeros_like(kmean_ref)
        drow = lax.broadcasted_iota(jnp.int32, (LANES, LANES), 0)
        dcol = lax.broadcasted_iota(jnp.int32, (LANES, LANES), 1)
        head_sum = jnp.where(drow // HEAD_DIM == dcol, 1.0, 0.0).astype(BF16)

        def max_norm2(x, best):
            rown = jnp.dot((x * x).astype(BF16), head_sum, preferred_element_type=F32)
            return jnp.maximum(best, jnp.max(rown, axis=0, keepdims=True))

        kn2 = jnp.zeros((1, LANES), F32)
        qn2 = jnp.zeros((1, LANES), F32)
        for j in range(nb):
            kj = k_ref[0, j * blk:(j + 1) * blk, :]
            kf = kj.astype(F32)
            kaug_ref[j, :, 0:LANES] = kj
            kaug_ref[j, :, LANES:2 * LANES] = jnp.where(lane == j, 1.0, 0.0).astype(BF16)
            kmean_ref[j:j + 1, :] = jnp.mean(kf, axis=0, keepdims=True)
            vt_ref[j] = v_ref[0, j * blk:(j + 1) * blk, :].astype(F32).T.astype(BF16)
            kn2 = max_norm2(kf, kn2)
            qn2 = max_norm2(q_ref[0, j * blk:(j + 1) * blk, :].astype(F32), qn2)
        bound2 = jnp.max(qn2 * kn2) * (NORM_SLACK * NORM_SLACK)
        flag_ref[0] = (bound2 < SCORE_BOUND * SCORE_BOUND).astype(jnp.int32)

    bounded = flag_ref[0] == 1
    off_i = pl.multiple_of(i * blk, blk)
    key = lax.broadcasted_iota(jnp.int32, (blk, blk), 0)
    qry = lax.broadcasted_iota(jnp.int32, (blk, blk), 1)

    def prologue():
        qt = q_ref[0, pl.ds(off_i, blk), :].astype(F32).T.astype(BF16)
        dim = lax.broadcasted_iota(jnp.int32, (PAIR, blk), 0)
        cand = lax.broadcasted_iota(jnp.int32, (nbp, blk), 0)
        kmean = kmean_ref[...]
        km_hi = kmean.astype(BF16)
        km_lo = (kmean - km_hi.astype(F32)).astype(BF16)
        pad = jnp.zeros((LANES - nbp, blk), BF16)
        qms, q_augs = [], []
        for head in range(2):
            in_head = (dim < HEAD_DIM) if head == 0 else (dim >= HEAD_DIM)
            qm = jnp.where(in_head, qt, jnp.zeros_like(qt))
            qms.append(qm)
            gate = (jnp.dot(km_hi, qm, preferred_element_type=F32)
                    + jnp.dot(km_lo, qm, preferred_element_type=F32))
            valid = cand < i
            gm = jnp.where(valid, gate, -jnp.inf)
            ahead = jnp.zeros((nbp, blk), jnp.int32)
            for n in range(nb - 1):
                gn = gm[n:n + 1, :]
                beats = (gn > gm) | ((gn == gm) & (cand > n))
                ahead = ahead + jnp.where(beats, 1, 0)
            sel = valid & (ahead < MOBA_TOPK)
            pen = jnp.where(sel, 0.0, MASK_NEG).astype(BF16)
            q_augs.append(jnp.concatenate([qm, pen, pad], axis=0))
        return qms, q_augs

    def k_own():
        return k_ref[0, pl.ds(off_i, blk), :]

    def finish(l_a, acc_a, l_b, acc_b):
        inv_a = 1.0 / jnp.sum(l_a, axis=0, keepdims=True)
        inv_b = 1.0 / jnp.sum(l_b, axis=0, keepdims=True)
        o = jnp.concatenate([acc_a * inv_a, acc_b * inv_b], axis=0)
        o_ref[0] = o.T.astype(o_ref.dtype)

    def pv(j, h, p):
        return jnp.dot(vt_ref[j, h * HEAD_DIM:(h + 1) * HEAD_DIM, :], p.astype(BF16),
                       preferred_element_type=F32)

    def colsum8(p):
        return jnp.sum(p.reshape(p.shape[0] // SUBLANES, SUBLANES, p.shape[1]), axis=0)

    @pl.when(bounded)
    def _():
        def attend(units, carry, qm, q_aug):
            scores = [jnp.dot(k_own(), qm(h), preferred_element_type=F32) if j is None
                      else jnp.dot(kaug_ref[j], q_aug(h), preferred_element_type=F32) for j, h in units]
            probs = [jnp.where(key <= qry, jnp.exp(s), 0.0) if j is None else jnp.exp(s)
                     for (j, h), s in zip(units, scores)]
            out = list(carry)
            for (j, h), p in zip(units, probs):
                out[2 * h] = colsum8(p) if out[2 * h] is None else out[2 * h] + colsum8(p)
            for (j, h), p in zip(units, probs):
                o = pv(i if j is None else j, h, p)
                out[2 * h + 1] = o if out[2 * h + 1] is None else out[2 * h + 1] + o
            return tuple(out)

        rem = i % BLOCK_GROUP
        for r in range(BLOCK_GROUP):
            @pl.when(rem == r)
            def _():
                qms, q_augs = prologue()
                units = [(None, 0), (None, 1)] + [(g, h) for g in range(r) for h in range(2)]
                res = attend(units, (None,) * 4, lambda h: qms[h], lambda h: q_augs[h])
                for h in range(2):
                    qaug_ref[h] = q_augs[h]
                    lsum_ref[h] = res[2 * h]
                    acc_ref[h] = res[2 * h + 1]

        def group(g, carry):
            j0 = rem + g * BLOCK_GROUP
            units = [(j0 + t, h) for t in range(BLOCK_GROUP) for h in range(2)]
            return attend(units, carry, None, lambda h: qaug_ref[h])

        finish(*lax.fori_loop(0, i // BLOCK_GROUP, group, (lsum_ref[0], acc_ref[0], lsum_ref[1], acc_ref[1])))

    @pl.when(jnp.logical_not(bounded))
    def _():
        qms, q_augs = prologue()
        init = []
        for head in range(2):
            s = jnp.dot(k_own(), qms[head], preferred_element_type=F32)
            s = jnp.where(key <= qry, s, -jnp.inf)
            m = jnp.max(s, axis=0, keepdims=True)
            p = jnp.exp(s - m)
            init += [m, jnp.sum(p, axis=0, keepdims=True), pv(i, head, p)]

        def body(j, carry):
            out = []
            for head in range(2):
                m, l, acc = carry[3 * head:3 * head + 3]
                s = jnp.dot(kaug_ref[j], q_augs[head], preferred_element_type=F32)
                m_new = jnp.maximum(m, jnp.max(s, axis=0, keepdims=True))
                alpha = jnp.exp(m - m_new)
                p = jnp.exp(s - m_new)
                out += [m_new, alpha * l + jnp.sum(p, axis=0, keepdims=True), alpha * acc + pv(j, head, p)]
            return tuple(out)

        res = lax.fori_loop(0, i, body, tuple(init))
        finish(res[1], res[2], res[4], res[5])


def _moba(q, k, v):
    bsz, seq, _ = q.shape
    blk = MOBA_BLOCK
    nb = seq // blk
    nbp = -(-nb // 16) * 16
    assert nbp <= LANES
    return pl.pallas_call(
        _moba_kernel,
        grid=(bsz, D_ATTN // PAIR, nb),
        in_specs=[pl.BlockSpec((1, seq, PAIR), lambda b, hp, i: (b, 0, hp))] * 3,
        out_specs=pl.BlockSpec((1, blk, PAIR), lambda b, hp, i: (b, i, hp)),
        out_shape=jax.ShapeDtypeStruct((bsz, seq, D_ATTN), BF16),
        scratch_shapes=[pltpu.VMEM((nb, blk, 2 * LANES), BF16), pltpu.VMEM((nb, PAIR, blk), BF16),
                        pltpu.VMEM((nbp, LANES), F32), pltpu.VMEM((2, 2 * LANES, blk), BF16),
                        pltpu.VMEM((2, SUBLANES, blk), F32), pltpu.VMEM((2, HEAD_DIM, blk), F32),
                        pltpu.SMEM((1,), jnp.int32)],
        compiler_params=_params("arbitrary", "arbitrary", "arbitrary"),
        name="moba",
    )(q, k, v)


def _s5_kernel(u_ref, wb_ref, are_ref, aim_ref, wc_ref, d_ref, wg_ref, bg_ref, gn_ref,
               o_ref, xh_ref, st_ref, tb_ref):
    n = SSM_GROUPS * SSM_STATE
    bsz, steps, _ = u_ref.shape
    halves = D_SSM // LANES

    @pl.when(pl.program_id(0) == 0)
    def _():
        st_ref[...] = jnp.zeros_like(st_ref)

    for b in range(bsz):
        for c in range(halves):
            tb_ref[c, pl.ds(b, steps, stride=bsz), :] = u_ref[b, :, c * LANES:(c + 1) * LANES]
    u = jnp.concatenate([tb_ref[c] for c in range(halves)], axis=1)
    xh_ref[...] = jnp.dot(u.astype(BF16), wb_ref[...], preferred_element_type=F32)
    ar, ai = are_ref[...], aim_ref[...]

    def step(t, carry):
        hr, hi = carry
        r0 = pl.multiple_of(t * bsz, bsz)
        xr = xh_ref[pl.ds(r0, bsz), 0:n]
        xi = xh_ref[pl.ds(r0, bsz), n:2 * n]
        nr = ar * hr - ai * hi + xr
        ni = ar * hi + ai * hr + xi
        xh_ref[pl.ds(r0, bsz), 0:n] = nr
        xh_ref[pl.ds(r0, bsz), n:2 * n] = ni
        return nr, ni

    hr, hi = lax.fori_loop(0, steps, step, (st_ref[:, 0:n], st_ref[:, n:2 * n]))
    st_ref[:, 0:n] = hr
    st_ref[:, n:2 * n] = hi

    y = jnp.dot(xh_ref[...].astype(BF16), wc_ref[...], preferred_element_type=F32) + d_ref[...] * u
    z = _gelu_tanh(y)
    gl = jnp.dot(z.astype(BF16), wg_ref[...], preferred_element_type=F32) + bg_ref[...]
    out = _rms(z * _sigmoid(gl), gn_ref[...])
    for c in range(halves):
        tb_ref[c] = out[:, c * LANES:(c + 1) * LANES]
    for b in range(bsz):
        for c in range(halves):
            o_ref[b, :, c * LANES:(c + 1) * LANES] = tb_ref[c, pl.ds(b, steps, stride=bsz), :].astype(o_ref.dtype)


def _s5(u, wb, a_re, a_im, wc, dsk, wg, bg, gn):
    bsz, seq, _ = u.shape
    rows = SSM_CHUNK * bsz
    n2 = 2 * SSM_GROUPS * SSM_STATE
    full = lambda r, c: pl.BlockSpec((r, c), lambda t: (0, 0))
    return pl.pallas_call(
        _s5_kernel,
        grid=(seq // SSM_CHUNK,),
        in_specs=[pl.BlockSpec((bsz, SSM_CHUNK, D_SSM), lambda t: (0, t, 0)),
                  full(D_SSM, n2), full(bsz, n2 // 2), full(bsz, n2 // 2), full(n2, D_SSM),
                  full(1, D_SSM), full(D_SSM, D_SSM), full(1, D_SSM), full(1, D_SSM)],
        out_specs=pl.BlockSpec((bsz, SSM_CHUNK, D_SSM), lambda t: (0, t, 0)),
        out_shape=jax.ShapeDtypeStruct((bsz, seq, D_SSM), BF16),
        scratch_shapes=[pltpu.VMEM((rows, n2), F32), pltpu.VMEM((bsz, n2), F32),
                        pltpu.VMEM((D_SSM // LANES, rows, LANES), F32)],
        compiler_params=_params("arbitrary"),
        name="s5",
    )(u, wb, a_re, a_im, wc, dsk, wg, bg, gn)


def _s5_matrices(lam_re, lam_im, log_dt, b_re, b_im, c_re, c_im, bsz):
    dt = jnp.exp(log_dt)[:, None]
    mag = jnp.exp(lam_re * dt)
    ab_re = mag * jnp.cos(lam_im * dt)
    ab_im = mag * jnp.sin(lam_im * dt)
    nr = ab_re - 1.0
    den = lam_re * lam_re + lam_im * lam_im
    f_re = (nr * lam_re + ab_im * lam_im) / den
    f_im = (ab_im * lam_re - nr * lam_im) / den
    bb_re = f_re[..., None] * b_re - f_im[..., None] * b_im
    bb_im = f_re[..., None] * b_im + f_im[..., None] * b_re
    eye = jnp.eye(SSM_GROUPS, dtype=F32)
    n = SSM_GROUPS * SSM_STATE
    wb = lambda bb: jnp.einsum('gpc,gh->gchp', bb, eye).reshape(D_SSM, n)
    wb_all = jnp.concatenate([wb(bb_re), wb(bb_im)], axis=1).astype(BF16)
    wc = lambda cc: jnp.einsum('gcp,gh->gphc', cc, eye).reshape(n, D_SSM)
    wc_all = jnp.concatenate([wc(c_re), -wc(c_im)], axis=0).astype(BF16)
    a_re = jnp.broadcast_to(ab_re.reshape(1, n), (bsz, n))
    a_im = jnp.broadcast_to(ab_im.reshape(1, n), (bsz, n))
    return wb_all, a_re, a_im, wc_all


def _mix_ffn_kernel(oa_ref, os_ref, oc_ref, h_ref, ga_ref, wo_ref, gf_ref, wgu_ref, wd_ref, gfin_ref,
                    o_ref, act_ref, *, d_ff, ff_chunk, final):
    an = _rms(oa_ref[0].astype(F32), ga_ref[...]).astype(BF16)
    mixed = jnp.concatenate([an, os_ref[0], oc_ref[0]], axis=1)
    h1 = h_ref[0] + jnp.dot(mixed, wo_ref[...], preferred_element_type=F32)
    hn = _rms(h1, gf_ref[...]).astype(BF16)
    for c in range(0, d_ff, ff_chunk):
        gate = jnp.dot(hn, wgu_ref[:, c:c + ff_chunk], preferred_element_type=F32)
        up = jnp.dot(hn, wgu_ref[:, d_ff + c:d_ff + c + ff_chunk], preferred_element_type=F32)
        act_ref[:, c:c + ff_chunk] = (gate * _sigmoid(gate) * up).astype(BF16)
    h2 = h1 + jnp.dot(act_ref[...], wd_ref[...], preferred_element_type=F32)
    if final:
        h2 = _rms(h2, gfin_ref[...])
    o_ref[0] = h2


def _mix_ffn(oa, os2, oc, h, ga, wo, gf, wgu, wd, gfin, final):
    bsz, seq, d = h.shape
    tm = TOK_TILE
    d_ff = wd.shape[0]
    tok = lambda n: pl.BlockSpec((1, tm, n), lambda i, b: (b, i, 0))
    full = lambda r, c: pl.BlockSpec((r, c), lambda i, b: (0, 0), pipeline_mode=pl.Buffered(1))
    return pl.pallas_call(
        functools.partial(_mix_ffn_kernel, d_ff=d_ff, ff_chunk=256, final=final),
        grid=(seq // tm, bsz),
        in_specs=[tok(D_ATTN), tok(D_SSM), tok(D_CONV), tok(d),
                  full(1, D_ATTN), full(d, d), full(1, d), full(d, 2 * d_ff), full(d_ff, d), full(1, d)],
        out_specs=tok(d),
        out_shape=jax.ShapeDtypeStruct((bsz, seq, d), F32),
        scratch_shapes=[pltpu.VMEM((tm, d_ff), BF16)],
        compiler_params=_params("arbitrary", "arbitrary"),
        name="mix_ffn",
    )(oa, os2, oc, h, ga, wo, gf, wgu, wd, gfin)


def _rope_tables(seq):
    half = ROT_DIM // 2
    inv = jnp.power(ROPE_THETA, -jnp.arange(0, ROT_DIM, 2, dtype=F32) / ROT_DIM)
    ang = jnp.arange(seq, dtype=F32)[:, None] * inv[None, :]
    cos, sin = jnp.cos(ang), jnp.sin(ang)
    ones = jnp.ones((seq, HEAD_DIM - ROT_DIM), F32)
    zeros = jnp.zeros((seq, HEAD_DIM - ROT_DIM), F32)
    zh = jnp.zeros((seq, half), F32)
    reps = LANES // HEAD_DIM
    rc = jnp.tile(jnp.concatenate([cos, cos, ones], axis=1), (1, reps))
    rs1 = jnp.tile(jnp.concatenate([-sin, zh, zeros], axis=1), (1, reps))
    rs2 = jnp.tile(jnp.concatenate([zh, sin, zeros], axis=1), (1, reps))
    return rc, rs1, rs2


def kernel(x, norm_mix_g, w_in, lambda_re, lambda_im, log_dt, b_re, b_im, c_re, c_im, d_skip, w_ssm_glu, b_ssm_glu, conv_w, conv_b, conv_ln_g, conv_ln_b, w_conv_pw2, b_conv_pw2, g_attn_out, g_ssm_out, g_conv_out, w_out, norm_ffn_g, w_gate_up, w_down, final_norm_g):
    bsz, seq, d = x.shape
    depth = w_in.shape[0]
    rc, rs1, rs2 = _rope_tables(seq)
    row = lambda a: a.reshape(1, -1)
    h = x
    for l in range(depth):
        cw = jnp.pad(conv_w[l], ((0, CONV_HALO - CONV_WIDTH), (0, 0)))
        q, k, v, u_ssm, o_conv = _inproj(h, row(norm_mix_g[l]), w_in[l].astype(BF16), rc, rs1, rs2,
                                         cw, row(conv_b[l]), row(conv_ln_g[l]), row(conv_ln_b[l]),
                                         w_conv_pw2[l].astype(BF16), row(b_conv_pw2[l]), row(g_conv_out[l]))
        o_attn = _moba(q, k, v)
        wb, a_re, a_im, wc = _s5_matrices(lambda_re[l], lambda_im[l], log_dt[l], b_re[l], b_im[l],
                                          c_re[l], c_im[l], bsz)
        o_ssm = _s5(u_ssm, wb, a_re, a_im, wc, row(d_skip[l]),
                    w_ssm_glu[l].astype(BF16), row(b_ssm_glu[l]), row(g_ssm_out[l]))
        h = _mix_ffn(o_attn, o_ssm, o_conv, h, row(g_attn_out[l]),
                     w_out[l].astype(BF16), row(norm_ffn_g[l]), w_gate_up[l].astype(BF16),
                     w_down[l].astype(BF16), row(final_norm_g), final=(l == depth - 1))
    return h
```

```python
import functools
import math

import jax
import jax.numpy as jnp
from jax import lax
from jax.experimental import pallas as pl
from jax.experimental.pallas import tpu as pltpu

F32 = jnp.float32
BF16 = jnp.bfloat16

ATTN_HEADS = 8
HEAD_DIM = 64
D_ATTN = ATTN_HEADS * HEAD_DIM
SSM_GROUP = 16
SSM_GROUPS = 16
SSM_STATE = 64
D_SSM = SSM_GROUP * SSM_GROUPS
D_CONV = 256
CONV_WIDTH = 31
ROT_DIM = HEAD_DIM // 4
ROPE_THETA = 500000.0
MOBA_BLOCK = 256
MOBA_TOPK = 3
RMS_EPS = 1e-6
LN_EPS = 1e-5

LANES = 128
SUBLANES = 8
PAIR = 2 * HEAD_DIM
MASK_NEG = -1e30
SCORE_BOUND = 40.0
NORM_SLACK = 1.02
BLOCK_GROUP = 8
VMEM_LIMIT = 56 * 1024 * 1024

TOK_TILE = 512
SSM_CHUNK = 128
CONV_ROWS = 64
CONV_HALO = 32


def _rms(x, g):
    return x * lax.rsqrt(jnp.mean(x * x, axis=-1, keepdims=True) + RMS_EPS) * g


def _sigmoid(x):
    return 1.0 / (1.0 + jnp.exp(-x))


def _gelu_tanh(x):
    c = math.sqrt(2.0 / math.pi)
    return 0.5 * x * (1.0 + jnp.tanh(c * (x + 0.044715 * (x * x * x))))


def _params(*sem):
    return pltpu.CompilerParams(dimension_semantics=sem, vmem_limit_bytes=VMEM_LIMIT)


def _inproj_kernel(h_ref, g_ref, w_ref, rc_ref, rs1_ref, rs2_ref,
                   cw_ref, cb_ref, lg_ref, lb_ref, pw_ref, pb_ref, gn_ref,
                   q_ref, k_ref, v_ref, us_ref, oc_ref, xx_ref, y_ref):
    tm = h_ref.shape[1]
    hn = _rms(h_ref[0], g_ref[...]).astype(BF16)

    def proj(lo, hi):
        return jnp.dot(hn, w_ref[:, lo:hi], preferred_element_type=F32)

    first = pl.program_id(1) == 0

    @pl.when(first)
    def _():
        xx_ref[0:CONV_HALO, :] = jnp.zeros((CONV_HALO, D_CONV), F32)

    @pl.when(jnp.logical_not(first))
    def _():
        xx_ref[0:CONV_HALO, :] = xx_ref[tm:tm + CONV_HALO, :]

    o = 3 * D_ATTN
    a = proj(o + D_SSM, o + D_SSM + D_CONV)
    g = proj(o + D_SSM + D_CONV, o + D_SSM + 2 * D_CONV)
    xx_ref[CONV_HALO:, :] = a * _sigmoid(g)
    rc, rs1, rs2 = rc_ref[...], rs1_ref[...], rs2_ref[...]

    def rope(t):
        return t * rc + pltpu.roll(t, LANES - ROT_DIM // 2, 1) * rs1 + pltpu.roll(t, ROT_DIM // 2, 1) * rs2

    q = proj(0, D_ATTN)
    k = proj(D_ATTN, 2 * D_ATTN)
    for c in range(D_ATTN // LANES):
        lo = c * LANES
        q_ref[0, :, lo:lo + LANES] = (rope(q[:, lo:lo + LANES]) * (HEAD_DIM ** -0.5)).astype(BF16)
        k_ref[0, :, lo:lo + LANES] = rope(k[:, lo:lo + LANES]).astype(BF16)
    v_ref[0] = proj(2 * D_ATTN, 3 * D_ATTN).astype(BF16)
    us_ref[0] = proj(o, o + D_SSM)

    base = CONV_HALO - (CONV_WIDTH - 1)
    for r in range(0, tm, CONV_ROWS):
        acc = jnp.broadcast_to(cb_ref[...], (CONV_ROWS, D_CONV))
        for s in range(SUBLANES):
            rows = CONV_ROWS + (SUBLANES if s else 0)
            z = None
            for off in range(s, CONV_HALO + 1, SUBLANES):
                j = off - base
                if 0 <= j < CONV_WIDTH:
                    term = cw_ref[j:j + 1, :] * xx_ref[r + off - s:r + off - s + rows, :]
                    z = term if z is None else z + term
            acc = acc + z[s:s + CONV_ROWS]
        y_ref[r:r + CONV_ROWS, :] = acc

    y = y_ref[...]
    mu = jnp.mean(y, axis=-1, keepdims=True)
    yc = y - mu
    yn = yc * lax.rsqrt(jnp.mean(yc * yc, axis=-1, keepdims=True) + LN_EPS) * lg_ref[...] + lb_ref[...]
    act = yn * _sigmoid(yn)
    out = jnp.dot(act.astype(BF16), pw_ref[...], preferred_element_type=F32) + pb_ref[...]
    oc_ref[0] = _rms(out, gn_ref[...]).astype(oc_ref.dtype)


def _inproj(h, g, w, rc, rs1, rs2, cw, cb, lg, lb, pw, pb, gn):
    bsz, seq, d = h.shape
    tm = TOK_TILE
    d_in = w.shape[1]
    tok = lambda n: pl.BlockSpec((1, tm, n), lambda b, i: (b, i, 0))
    full = lambda r, c: pl.BlockSpec((r, c), lambda b, i: (0, 0))
    tab = pl.BlockSpec((tm, LANES), lambda b, i: (i, 0))
    return pl.pallas_call(
        _inproj_kernel,
        grid=(bsz, seq // tm),
        in_specs=[tok(d), full(1, d), full(d, d_in), tab, tab, tab,
                  full(CONV_HALO, D_CONV), full(1, D_CONV), full(1, D_CONV), full(1, D_CONV),
                  full(D_CONV, D_CONV), full(1, D_CONV), full(1, D_CONV)],
        out_specs=[tok(D_ATTN), tok(D_ATTN), tok(D_ATTN), tok(D_SSM), tok(D_CONV)],
        out_shape=[jax.ShapeDtypeStruct((bsz, seq, D_ATTN), BF16)] * 3
        + [jax.ShapeDtypeStruct((bsz, seq, D_SSM), F32),
           jax.ShapeDtypeStruct((bsz, seq, D_CONV), BF16)],
        scratch_shapes=[pltpu.VMEM((tm + CONV_HALO, D_CONV), F32), pltpu.VMEM((tm, D_CONV), F32)],
        compiler_params=_params("arbitrary", "arbitrary"),
        name="inproj",
    )(h, g, w, rc, rs1, rs2, cw, cb, lg, lb, pw, pb, gn)


def _moba_kernel(q_ref, k_ref, v_ref, o_ref, kaug_ref, vt_ref, kmean_ref, qaug_ref, lsum_ref, acc_ref,
                 flag_ref):
    i = pl.program_id(2)
    blk = MOBA_BLOCK
    nb = kaug_ref.shape[0]
    nbp = kmean_ref.shape[0]

    @pl.when(i == 0)
    def _():
        lane = lax.broadcasted_iota(jnp.int32, (blk, LANES), 1)
        kmean_ref[...] = jnp.zeros_like(kmean_ref)
        drow = lax.broadcasted_iota(jnp.int32, (LANES, LANES), 0)
        dcol = lax.broadcasted_iota(jnp.int32, (LANES, LANES), 1)
        head_sum = jnp.where(drow // HEAD_DIM == dcol, 1.0, 0.0).astype(BF16)

        def max_norm2(x, best):
            rown = jnp.dot((x * x).astype(BF16), head_sum, preferred_element_type=F32)
            return jnp.maximum(best, jnp.max(rown, axis=0, keepdims=True))

        kn2 = jnp.zeros((1, LANES), F32)
        qn2 = jnp.zeros((1, LANES), F32)
        for j in range(nb):
            kj = k_ref[0, j * blk:(j + 1) * blk, :]
            kf = kj.astype(F32)
            kaug_ref[j, :, 0:LANES] = kj
            kaug_ref[j, :, LANES:2 * LANES] = jnp.where(lane == j, 1.0, 0.0).astype(BF16)
            kmean_ref[j:j + 1, :] = jnp.mean(kf, axis=0, keepdims=True)
            vt_ref[j] = v_ref[0, j * blk:(j + 1) * blk, :].astype(F32).T.astype(BF16)
            kn2 = max_norm2(kf, kn2)
            qn2 = max_norm2(q_ref[0, j * blk:(j + 1) * blk, :].astype(F32), qn2)
        bound2 = jnp.max(qn2 * kn2) * (NORM_SLACK * NORM_SLACK)
        flag_ref[0] = (bound2 < SCORE_BOUND * SCORE_BOUND).astype(jnp.int32)

    bounded = flag_ref[0] == 1
    off_i = pl.multiple_of(i * blk, blk)
    key = lax.broadcasted_iota(jnp.int32, (blk, blk), 0)
    qry = lax.broadcasted_iota(jnp.int32, (blk, blk), 1)

    def prologue():
        qt = q_ref[0, pl.ds(off_i, blk), :].astype(F32).T.astype(BF16)
        dim = lax.broadcasted_iota(jnp.int32, (PAIR, blk), 0)
        cand = lax.broadcasted_iota(jnp.int32, (nbp, blk), 0)
        kmean = kmean_ref[...]
        km_hi = kmean.astype(BF16)
        km_lo = (kmean - km_hi.astype(F32)).astype(BF16)
        pad = jnp.zeros((LANES - nbp, blk), BF16)
        qms, q_augs = [], []
        for head in range(2):
            in_head = (dim < HEAD_DIM) if head == 0 else (dim >= HEAD_DIM)
            qm = jnp.where(in_head, qt, jnp.zeros_like(qt))
            qms.append(qm)
            gate = (jnp.dot(km_hi, qm, preferred_element_type=F32)
                    + jnp.dot(km_lo, qm, preferred_element_type=F32))
            valid = cand < i
            gm = jnp.where(valid, gate, -jnp.inf)
            ahead = jnp.zeros((nbp, blk), jnp.int32)
            for n in range(nb - 1):
                gn = gm[n:n + 1, :]
                beats = (gn > gm) | ((gn == gm) & (cand > n))
                ahead = ahead + jnp.where(beats, 1, 0)
            sel = valid & (ahead < MOBA_TOPK)
            pen = jnp.where(sel, 0.0, MASK_NEG).astype(BF16)
            q_augs.append(jnp.concatenate([qm, pen, pad], axis=0))
        return qms, q_augs

    def k_own():
        return k_ref[0, pl.ds(off_i, blk), :]

    def finish(l_a, acc_a, l_b, acc_b):
        inv_a = 1.0 / jnp.sum(l_a, axis=0, keepdims=True)
        inv_b = 1.0 / jnp.sum(l_b, axis=0, keepdims=True)
        o = jnp.concatenate([acc_a * inv_a, acc_b * inv_b], axis=0)
        o_ref[0] = o.T.astype(o_ref.dtype)

    def pv(j, h, p):
        return jnp.dot(vt_ref[j, h * HEAD_DIM:(h + 1) * HEAD_DIM, :], p.astype(BF16),
                       preferred_element_type=F32)

    def colsum8(p):
        return jnp.sum(p.reshape(p.shape[0] // SUBLANES, SUBLANES, p.shape[1]), axis=0)

    @pl.when(bounded)
    def _():
        def attend(units, carry, qm, q_aug):
            scores = [jnp.dot(k_own(), qm(h), preferred_element_type=F32) if j is None
                      else jnp.dot(kaug_ref[j], q_aug(h), preferred_element_type=F32) for j, h in units]
            probs = [jnp.where(key <= qry, jnp.exp(s), 0.0) if j is None else jnp.exp(s)
                     for (j, h), s in zip(units, scores)]
            out = list(carry)
            for (j, h), p in zip(units, probs):
                out[2 * h] = colsum8(p) if out[2 * h] is None else out[2 * h] + colsum8(p)
            for (j, h), p in zip(units, probs):
                o = pv(i if j is None else j, h, p)
                out[2 * h + 1] = o if out[2 * h + 1] is None else out[2 * h + 1] + o
            return tuple(out)

        rem = i % BLOCK_GROUP
        for r in range(BLOCK_GROUP):
            @pl.when(rem == r)
            def _():
                qms, q_augs = prologue()
                units = [(None, 0), (None, 1)] + [(g, h) for g in range(r) for h in range(2)]
                res = attend(units, (None,) * 4, lambda h: qms[h], lambda h: q_augs[h])
                for h in range(2):
                    qaug_ref[h] = q_augs[h]
                    lsum_ref[h] = res[2 * h]
                    acc_ref[h] = res[2 * h + 1]

        def group(g, carry):
            j0 = rem + g * BLOCK_GROUP
            units = [(j0 + t, h) for t in range(BLOCK_GROUP) for h in range(2)]
            return attend(units, carry, None, lambda h: qaug_ref[h])

        finish(*lax.fori_loop(0, i // BLOCK_GROUP, group, (lsum_ref[0], acc_ref[0], lsum_ref[1], acc_ref[1])))

    @pl.when(jnp.logical_not(bounded))
    def _():
        qms, q_augs = prologue()
        init = []
        for head in range(2):
            s = jnp.dot(k_own(), qms[head], preferred_element_type=F32)
            s = jnp.where(key <= qry, s, -jnp.inf)
            m = jnp.max(s, axis=0, keepdims=True)
            p = jnp.exp(s - m)
            init += [m, jnp.sum(p, axis=0, keepdims=True), pv(i, head, p)]

        def body(j, carry):
            out = []
            for head in range(2):
                m, l, acc = carry[3 * head:3 * head + 3]
                s = jnp.dot(kaug_ref[j], q_augs[head], preferred_element_type=F32)
                m_new = jnp.maximum(m, jnp.max(s, axis=0, keepdims=True))
                alpha = jnp.exp(m - m_new)
                p = jnp.exp(s - m_new)
                out += [m_new, alpha * l + jnp.sum(p, axis=0, keepdims=True), alpha * acc + pv(j, head, p)]
            return tuple(out)

        res = lax.fori_loop(0, i, body, tuple(init))
        finish(res[1], res[2], res[4], res[5])


def _moba(q, k, v):
    bsz, seq, _ = q.shape
    blk = MOBA_BLOCK
    nb = seq // blk
    nbp = -(-nb // 16) * 16
    assert nbp <= LANES
    return pl.pallas_call(
        _moba_kernel,
        grid=(bsz, D_ATTN // PAIR, nb),
        in_specs=[pl.BlockSpec((1, seq, PAIR), lambda b, hp, i: (b, 0, hp))] * 3,
        out_specs=pl.BlockSpec((1, blk, PAIR), lambda b, hp, i: (b, i, hp)),
        out_shape=jax.ShapeDtypeStruct((bsz, seq, D_ATTN), BF16),
        scratch_shapes=[pltpu.VMEM((nb, blk, 2 * LANES), BF16), pltpu.VMEM((nb, PAIR, blk), BF16),
                        pltpu.VMEM((nbp, LANES), F32), pltpu.VMEM((2, 2 * LANES, blk), BF16),
                        pltpu.VMEM((2, SUBLANES, blk), F32), pltpu.VMEM((2, HEAD_DIM, blk), F32),
                        pltpu.SMEM((1,), jnp.int32)],
        compiler_params=_params("arbitrary", "arbitrary", "arbitrary"),
        name="moba",
    )(q, k, v)


def _s5_kernel(u_ref, wb_ref, are_ref, aim_ref, wc_ref, d_ref, wg_ref, bg_ref, gn_ref,
               o_ref, xh_ref, st_ref, tb_ref):
    n = SSM_GROUPS * SSM_STATE
    bsz, steps, _ = u_ref.shape
    halves = D_SSM // LANES

    @pl.when(pl.program_id(0) == 0)
    def _():
        st_ref[...] = jnp.zeros_like(st_ref)

    for b in range(bsz):
        for c in range(halves):
            tb_ref[c, pl.ds(b, steps, stride=bsz), :] = u_ref[b, :, c * LANES:(c + 1) * LANES]
    u = jnp.concatenate([tb_ref[c] for c in range(halves)], axis=1)
    ub = u.astype(BF16)
    half = u.shape[0] // 2
    row_halves = (slice(0, half), slice(half, 2 * half))
    for rs in row_halves:
        xh_ref[rs, :] = jnp.dot(ub[rs], wb_ref[...], preferred_element_type=F32)
    ar, ai = are_ref[...], aim_ref[...]

    def step(t, carry):
        hr, hi = carry
        r0 = pl.multiple_of(t * bsz, bsz)
        xr = xh_ref[pl.ds(r0, bsz), 0:n]
        xi = xh_ref[pl.ds(r0, bsz), n:2 * n]
        nr = ar * hr - ai * hi + xr
        ni = ar * hi + ai * hr + xi
        xh_ref[pl.ds(r0, bsz), 0:n] = nr
        xh_ref[pl.ds(r0, bsz), n:2 * n] = ni
        return nr, ni

    hr, hi = lax.fori_loop(0, steps, step, (st_ref[:, 0:n], st_ref[:, n:2 * n]))
    st_ref[:, 0:n] = hr
    st_ref[:, n:2 * n] = hi

    ys = [jnp.dot(xh_ref[rs, :].astype(BF16), wc_ref[...], preferred_element_type=F32) for rs in row_halves]
    zs = [_gelu_tanh(y + d_ref[...] * u[rs]) for y, rs in zip(ys, row_halves)]
    gls = [jnp.dot(z.astype(BF16), wg_ref[...], preferred_element_type=F32) + bg_ref[...] for z in zs]
    for z, gl, rs in zip(zs, gls, row_halves):
        out = _rms(z * _sigmoid(gl), gn_ref[...])
        for c in range(halves):
            tb_ref[c, rs, :] = out[:, c * LANES:(c + 1) * LANES]
    for b in range(bsz):
        for c in range(halves):
            o_ref[b, :, c * LANES:(c + 1) * LANES] = tb_ref[c, pl.ds(b, steps, stride=bsz), :].astype(o_ref.dtype)


def _s5(u, wb, a_re, a_im, wc, dsk, wg, bg, gn):
    bsz, seq, _ = u.shape
    rows = SSM_CHUNK * bsz
    n2 = 2 * SSM_GROUPS * SSM_STATE
    full = lambda r, c: pl.BlockSpec((r, c), lambda t: (0, 0))
    return pl.pallas_call(
        _s5_kernel,
        grid=(seq // SSM_CHUNK,),
        in_specs=[pl.BlockSpec((bsz, SSM_CHUNK, D_SSM), lambda t: (0, t, 0)),
                  full(D_SSM, n2), full(bsz, n2 // 2), full(bsz, n2 // 2), full(n2, D_SSM),
                  full(1, D_SSM), full(D_SSM, D_SSM), full(1, D_SSM), full(1, D_SSM)],
        out_specs=pl.BlockSpec((bsz, SSM_CHUNK, D_SSM), lambda t: (0, t, 0)),
        out_shape=jax.ShapeDtypeStruct((bsz, seq, D_SSM), BF16),
        scratch_shapes=[pltpu.VMEM((rows, n2), F32), pltpu.VMEM((bsz, n2), F32),
                        pltpu.VMEM((D_SSM // LANES, rows, LANES), F32)],
        compiler_params=_params("arbitrary"),
        name="s5",
    )(u, wb, a_re, a_im, wc, dsk, wg, bg, gn)


def _s5_matrices(lam_re, lam_im, log_dt, b_re, b_im, c_re, c_im, bsz):
    dt = jnp.exp(log_dt)[:, None]
    mag = jnp.exp(lam_re * dt)
    ab_re = mag * jnp.cos(lam_im * dt)
    ab_im = mag * jnp.sin(lam_im * dt)
    nr = ab_re - 1.0
    den = lam_re * lam_re + lam_im * lam_im
    f_re = (nr * lam_re + ab_im * lam_im) / den
    f_im = (ab_im * lam_re - nr * lam_im) / den
    bb_re = f_re[..., None] * b_re - f_im[..., None] * b_im
    bb_im = f_re[..., None] * b_im + f_im[..., None] * b_re
    eye = jnp.eye(SSM_GROUPS, dtype=F32)
    n = SSM_GROUPS * SSM_STATE
    wb = lambda bb: jnp.einsum('gpc,gh->gchp', bb, eye).reshape(D_SSM, n)
    wb_all = jnp.concatenate([wb(bb_re), wb(bb_im)], axis=1).astype(BF16)
    wc = lambda cc: jnp.einsum('gcp,gh->gphc', cc, eye).reshape(n, D_SSM)
    wc_all = jnp.concatenate([wc(c_re), -wc(c_im)], axis=0).astype(BF16)
    a_re = jnp.broadcast_to(ab_re.reshape(1, n), (bsz, n))
    a_im = jnp.broadcast_to(ab_im.reshape(1, n), (bsz, n))
    return wb_all, a_re, a_im, wc_all


def _mix_ffn_kernel(oa_ref, os_ref, oc_ref, h_ref, ga_ref, wo_ref, gf_ref, wgu_ref, wd_ref, gfin_ref,
                    o_ref, act_ref, *, d_ff, ff_chunk, final):
    an = _rms(oa_ref[0].astype(F32), ga_ref[...]).astype(BF16)
    mixed = jnp.concatenate([an, os_ref[0], oc_ref[0]], axis=1)
    h1 = h_ref[0] + jnp.dot(mixed, wo_ref[...], preferred_element_type=F32)
    hn = _rms(h1, gf_ref[...]).astype(BF16)
    for c in range(0, d_ff, ff_chunk):
        gate = jnp.dot(hn, wgu_ref[:, c:c + ff_chunk], preferred_element_type=F32)
        up = jnp.dot(hn, wgu_ref[:, d_ff + c:d_ff + c + ff_chunk], preferred_element_type=F32)
        act_ref[:, c:c + ff_chunk] = (gate * _sigmoid(gate) * up).astype(BF16)
    h2 = h1 + jnp.dot(act_ref[...], wd_ref[...], preferred_element_type=F32)
    if final:
        h2 = _rms(h2, gfin_ref[...])
    o_ref[0] = h2


def _mix_ffn(oa, os2, oc, h, ga, wo, gf, wgu, wd, gfin, final):
    bsz, seq, d = h.shape
    tm = TOK_TILE
    d_ff = wd.shape[0]
    tok = lambda n: pl.BlockSpec((1, tm, n), lambda i, b: (b, i, 0))
    full = lambda r, c: pl.BlockSpec((r, c), lambda i, b: (0, 0), pipeline_mode=pl.Buffered(1))
    return pl.pallas_call(
        functools.partial(_mix_ffn_kernel, d_ff=d_ff, ff_chunk=256, final=final),
        grid=(seq // tm, bsz),
        in_specs=[tok(D_ATTN), tok(D_SSM), tok(D_CONV), tok(d),
                  full(1, D_ATTN), full(d, d), full(1, d), full(d, 2 * d_ff), full(d_ff, d), full(1, d)],
        out_specs=tok(d),
        out_shape=jax.ShapeDtypeStruct((bsz, seq, d), F32),
        scratch_shapes=[pltpu.VMEM((tm, d_ff), BF16)],
        compiler_params=_params("arbitrary", "arbitrary"),
        name="mix_ffn",
    )(oa, os2, oc, h, ga, wo, gf, wgu, wd, gfin)


def _rope_tables(seq):
    half = ROT_DIM // 2
    inv = jnp.power(ROPE_THETA, -jnp.arange(0, ROT_DIM, 2, dtype=F32) / ROT_DIM)
    ang = jnp.arange(seq, dtype=F32)[:, None] * inv[None, :]
    cos, sin = jnp.cos(ang), jnp.sin(ang)
    ones = jnp.ones((seq, HEAD_DIM - ROT_DIM), F32)
    zeros = jnp.zeros((seq, HEAD_DIM - ROT_DIM), F32)
    zh = jnp.zeros((seq, half), F32)
    reps = LANES // HEAD_DIM
    rc = jnp.tile(jnp.concatenate([cos, cos, ones], axis=1), (1, reps))
    rs1 = jnp.tile(jnp.concatenate([-sin, zh, zeros], axis=1), (1, reps))
    rs2 = jnp.tile(jnp.concatenate([zh, sin, zeros], axis=1), (1, reps))
    return rc, rs1, rs2


def kernel(x, norm_mix_g, w_in, lambda_re, lambda_im, log_dt, b_re, b_im, c_re, c_im, d_skip, w_ssm_glu, b_ssm_glu, conv_w, conv_b, conv_ln_g, conv_ln_b, w_conv_pw2, b_conv_pw2, g_attn_out, g_ssm_out, g_conv_out, w_out, norm_ffn_g, w_gate_up, w_down, final_norm_g):
    bsz, seq, d = x.shape
    depth = w_in.shape[0]
    rc, rs1, rs2 = _rope_tables(seq)
    row = lambda a: a.reshape(1, -1)
    h = x
    for l in range(depth):
        cw = jnp.pad(conv_w[l], ((0, CONV_HALO - CONV_WIDTH), (0, 0)))
        q, k, v, u_ssm, o_conv = _inproj(h, row(norm_mix_g[l]), w_in[l].astype(BF16), rc, rs1, rs2,
                                         cw, row(conv_b[l]), row(conv_ln_g[l]), row(conv_ln_b[l]),
                                         w_conv_pw2[l].astype(BF16), row(b_conv_pw2[l]), row(g_conv_out[l]))
        o_attn = _moba(q, k, v)
        wb, a_re, a_im, wc = _s5_matrices(lambda_re[l], lambda_im[l], log_dt[l], b_re[l], b_im[l],
                                          c_re[l], c_im[l], bsz)
        o_ssm = _s5(u_ssm, wb, a_re, a_im, wc, row(d_skip[l]),
                    w_ssm_glu[l].astype(BF16), row(b_ssm_glu[l]), row(g_ssm_out[l]))
        h = _mix_ffn(o_attn, o_ssm, o_conv, h, row(g_attn_out[l]),
                     w_out[l].astype(BF16), row(norm_ffn_g[l]), w_gate_up[l].astype(BF16),
                     w_down[l].astype(BF16), row(final_norm_g), final=(l == depth - 1))
    return h
```

```python
import functools
import math

import jax
import jax.numpy as jnp
from jax import lax
from jax.experimental import pallas as pl
from jax.experimental.pallas import tpu as pltpu

F32 = jnp.float32
BF16 = jnp.bfloat16

ATTN_HEADS = 8
HEAD_DIM = 64
D_ATTN = ATTN_HEADS * HEAD_DIM
SSM_GROUP = 16
SSM_GROUPS = 16
SSM_STATE = 64
D_SSM = SSM_GROUP * SSM_GROUPS
D_CONV = 256
CONV_WIDTH = 31
ROT_DIM = HEAD_DIM // 4
ROPE_THETA = 500000.0
MOBA_BLOCK = 256
MOBA_TOPK = 3
RMS_EPS = 1e-6
LN_EPS = 1e-5

LANES = 128
SUBLANES = 8
PAIR = 2 * HEAD_DIM
MASK_NEG = -1e30
SCORE_BOUND = 40.0
NORM_SLACK = 1.02
BLOCK_GROUP = 16
VMEM_LIMIT = 56 * 1024 * 1024

TOK_TILE = 512
FFN_TILE = 1024
SSM_CHUNK = 128
CONV_ROWS = 64
CONV_HALO = 32


def _rms(x, g):
    return x * lax.rsqrt(jnp.mean(x * x, axis=-1, keepdims=True) + RMS_EPS) * g


def _sigmoid(x):
    return 1.0 / (1.0 + jnp.exp(-x))


def _gelu_tanh(x):
    c = math.sqrt(2.0 / math.pi)
    return 0.5 * x * (1.0 + jnp.tanh(c * (x + 0.044715 * (x * x * x))))


def _params(*sem):
    return pltpu.CompilerParams(dimension_semantics=sem, vmem_limit_bytes=VMEM_LIMIT)


def _inproj_kernel(h_ref, g_ref, w_ref, rc_ref, rs1_ref, rs2_ref,
                   cw_ref, cb_ref, lg_ref, lb_ref, pw_ref, pb_ref, gn_ref,
                   q_ref, k_ref, v_ref, us_ref, oc_ref, xx_ref, y_ref):
    tm = h_ref.shape[1]
    hn = _rms(h_ref[0], g_ref[...]).astype(BF16)

    def proj(lo, hi):
        return jnp.dot(hn, w_ref[:, lo:hi], preferred_element_type=F32)

    first = pl.program_id(1) == 0

    @pl.when(first)
    def _():
        xx_ref[0:CONV_HALO, :] = jnp.zeros((CONV_HALO, D_CONV), F32)

    @pl.when(jnp.logical_not(first))
    def _():
        xx_ref[0:CONV_HALO, :] = xx_ref[tm:tm + CONV_HALO, :]

    o = 3 * D_ATTN
    a = proj(o + D_SSM, o + D_SSM + D_CONV)
    g = proj(o + D_SSM + D_CONV, o + D_SSM + 2 * D_CONV)
    xx_ref[CONV_HALO:, :] = a * _sigmoid(g)
    rc, rs1, rs2 = rc_ref[...], rs1_ref[...], rs2_ref[...]

    def rope(t):
        return t * rc + pltpu.roll(t, LANES - ROT_DIM // 2, 1) * rs1 + pltpu.roll(t, ROT_DIM // 2, 1) * rs2

    q = proj(0, D_ATTN)
    k = proj(D_ATTN, 2 * D_ATTN)
    for c in range(D_ATTN // LANES):
        lo = c * LANES
        q_ref[0, :, lo:lo + LANES] = (rope(q[:, lo:lo + LANES]) * (HEAD_DIM ** -0.5)).astype(BF16)
        k_ref[0, :, lo:lo + LANES] = rope(k[:, lo:lo + LANES]).astype(BF16)
    v_ref[0] = proj(2 * D_ATTN, 3 * D_ATTN).astype(BF16)
    us_ref[0] = proj(o, o + D_SSM)

    base = CONV_HALO - (CONV_WIDTH - 1)
    for r in range(0, tm, CONV_ROWS):
        acc = jnp.broadcast_to(cb_ref[...], (CONV_ROWS, D_CONV))
        for s in range(SUBLANES):
            rows = CONV_ROWS + (SUBLANES if s else 0)
            z = None
            for off in range(s, CONV_HALO + 1, SUBLANES):
                j = off - base
                if 0 <= j < CONV_WIDTH:
                    term = cw_ref[j:j + 1, :] * xx_ref[r + off - s:r + off - s + rows, :]
                    z = term if z is None else z + term
            acc = acc + z[s:s + CONV_ROWS]
        y_ref[r:r + CONV_ROWS, :] = acc

    y = y_ref[...]
    mu = jnp.mean(y, axis=-1, keepdims=True)
    yc = y - mu
    yn = yc * lax.rsqrt(jnp.mean(yc * yc, axis=-1, keepdims=True) + LN_EPS) * lg_ref[...] + lb_ref[...]
    act = yn * _sigmoid(yn)
    out = jnp.dot(act.astype(BF16), pw_ref[...], preferred_element_type=F32) + pb_ref[...]
    oc_ref[0] = _rms(out, gn_ref[...]).astype(oc_ref.dtype)


def _inproj(h, g, w, layer, rc, rs1, rs2, cw, cb, lg, lb, pw, pb, gn):
    bsz, seq, d = h.shape
    tm = TOK_TILE
    d_in = w.shape[2]
    tok = lambda n: pl.BlockSpec((1, tm, n), lambda b, i: (b, i, 0))
    full = lambda r, c: pl.BlockSpec((r, c), lambda b, i: (0, 0))
    tab = pl.BlockSpec((tm, LANES), lambda b, i: (i, 0))
    return pl.pallas_call(
        _inproj_kernel,
        grid=(bsz, seq // tm),
        in_specs=[tok(d), full(1, d), pl.BlockSpec((None, d, d_in), lambda b, i: (layer, 0, 0)), tab, tab, tab,
                  full(CONV_HALO, D_CONV), full(1, D_CONV), full(1, D_CONV), full(1, D_CONV),
                  full(D_CONV, D_CONV), full(1, D_CONV), full(1, D_CONV)],
        out_specs=[tok(D_ATTN), tok(D_ATTN), tok(D_ATTN), tok(D_SSM), tok(D_CONV)],
        out_shape=[jax.ShapeDtypeStruct((bsz, seq, D_ATTN), BF16)] * 3
        + [jax.ShapeDtypeStruct((bsz, seq, D_SSM), F32),
           jax.ShapeDtypeStruct((bsz, seq, D_CONV), BF16)],
        scratch_shapes=[pltpu.VMEM((tm + CONV_HALO, D_CONV), F32), pltpu.VMEM((tm, D_CONV), F32)],
        compiler_params=_params("arbitrary", "arbitrary"),
        name="inproj",
    )(h, g, w, rc, rs1, rs2, cw, cb, lg, lb, pw, pb, gn)


def _moba_kernel(q_ref, k_ref, v_ref, o_ref, kaug_ref, vt_ref, kmean_ref, qaug_ref, lsum_ref, acc_ref,
                 flag_ref):
    i = pl.program_id(2)
    blk = MOBA_BLOCK
    nb = kaug_ref.shape[0]
    nbp = kmean_ref.shape[0]

    @pl.when(i == 0)
    def _():
        lane = lax.broadcasted_iota(jnp.int32, (blk, LANES), 1)
        kmean_ref[...] = jnp.zeros_like(kmean_ref)
        drow = lax.broadcasted_iota(jnp.int32, (LANES, LANES), 0)
        dcol = lax.broadcasted_iota(jnp.int32, (LANES, LANES), 1)
        head_sum = jnp.where(drow // HEAD_DIM == dcol, 1.0, 0.0).astype(BF16)

        def max_norm2(x, best):
            rown = jnp.dot((x * x).astype(BF16), head_sum, preferred_element_type=F32)
            return jnp.maximum(best, jnp.max(rown, axis=0, keepdims=True))

        kn2 = jnp.zeros((1, LANES), F32)
        qn2 = jnp.zeros((1, LANES), F32)
        for j in range(nb):
            kj = k_ref[0, j * blk:(j + 1) * blk, :]
            kf = kj.astype(F32)
            kaug_ref[j, :, 0:LANES] = kj
            kaug_ref[j, :, LANES:2 * LANES] = jnp.where(lane == j, 1.0, 0.0).astype(BF16)
            kmean_ref[j:j + 1, :] = jnp.mean(kf, axis=0, keepdims=True)
            vt_ref[j] = v_ref[0, j * blk:(j + 1) * blk, :].astype(F32).T.astype(BF16)
            kn2 = max_norm2(kf, kn2)
            qn2 = max_norm2(q_ref[0, j * blk:(j + 1) * blk, :].astype(F32), qn2)
        bound2 = jnp.max(qn2 * kn2) * (NORM_SLACK * NORM_SLACK)
        flag_ref[0] = (bound2 < SCORE_BOUND * SCORE_BOUND).astype(jnp.int32)

    bounded = flag_ref[0] == 1
    off_i = pl.multiple_of(i * blk, blk)
    key = lax.broadcasted_iota(jnp.int32, (blk, blk), 0)
    qry = lax.broadcasted_iota(jnp.int32, (blk, blk), 1)

    def prologue():
        qt = q_ref[0, pl.ds(off_i, blk), :].astype(F32).T.astype(BF16)
        dim = lax.broadcasted_iota(jnp.int32, (PAIR, blk), 0)
        cand = lax.broadcasted_iota(jnp.int32, (nbp, blk), 0)
        kmean = kmean_ref[...]
        km_hi = kmean.astype(BF16)
        km_lo = (kmean - km_hi.astype(F32)).astype(BF16)
        pad = jnp.zeros((LANES - nbp, blk), BF16)
        qms, q_augs = [], []
        for head in range(2):
            in_head = (dim < HEAD_DIM) if head == 0 else (dim >= HEAD_DIM)
            qm = jnp.where(in_head, qt, jnp.zeros_like(qt))
            qms.append(qm)
            gate = (jnp.dot(km_hi, qm, preferred_element_type=F32)
                    + jnp.dot(km_lo, qm, preferred_element_type=F32))
            valid = cand < i
            gm = jnp.where(valid, gate, -jnp.inf)
            ahead = jnp.zeros((nbp, blk), jnp.int32)
            for n in range(nb - 1):
                gn = gm[n:n + 1, :]
                beats = (gn > gm) | ((gn == gm) & (cand > n))
                ahead = ahead + jnp.where(beats, 1, 0)
            sel = valid & (ahead < MOBA_TOPK)
            pen = jnp.where(sel, 0.0, MASK_NEG).astype(BF16)
            q_augs.append(jnp.concatenate([qm, pen, pad], axis=0))
        return qms, q_augs

    def k_own():
        return k_ref[0, pl.ds(off_i, blk), :]

    def finish(l_a, acc_a, l_b, acc_b):
        inv_a = 1.0 / jnp.sum(l_a, axis=0, keepdims=True)
        inv_b = 1.0 / jnp.sum(l_b, axis=0, keepdims=True)
        o = jnp.concatenate([acc_a * inv_a, acc_b * inv_b], axis=0)
        o_ref[0] = o.T.astype(o_ref.dtype)

    def pv(j, h, p):
        return jnp.dot(vt_ref[j, h * HEAD_DIM:(h + 1) * HEAD_DIM, :], p.astype(BF16),
                       preferred_element_type=F32)

    def colsum8(p):
        return jnp.sum(p.reshape(p.shape[0] // SUBLANES, SUBLANES, p.shape[1]), axis=0)

    @pl.when(bounded)
    def _():
        def attend(units, carry, qm, q_aug):
            scores = [jnp.dot(k_own(), qm(h), preferred_element_type=F32) if j is None
                      else jnp.dot(kaug_ref[j], q_aug(h), preferred_element_type=F32) for j, h in units]
            probs = [jnp.where(key <= qry, jnp.exp(s), 0.0) if j is None else jnp.exp(s)
                     for (j, h), s in zip(units, scores)]
            out = list(carry)
            for (j, h), p in zip(units, probs):
                out[2 * h] = colsum8(p) if out[2 * h] is None else out[2 * h] + colsum8(p)
            for (j, h), p in zip(units, probs):
                o = pv(i if j is None else j, h, p)
                out[2 * h + 1] = o if out[2 * h + 1] is None else out[2 * h + 1] + o
            return tuple(out)

        rem = i % BLOCK_GROUP
        for r in range(BLOCK_GROUP):
            @pl.when(rem == r)
            def _():
                qms, q_augs = prologue()
                units = [(None, 0), (None, 1)] + [(g, h) for g in range(r) for h in range(2)]
                res = attend(units, (None,) * 4, lambda h: qms[h], lambda h: q_augs[h])
                for h in range(2):
                    qaug_ref[h] = q_augs[h]
                    lsum_ref[h] = res[2 * h]
                    acc_ref[h] = res[2 * h + 1]

        def group(g, carry):
            j0 = rem + g * BLOCK_GROUP
            units = [(j0 + t, h) for t in range(BLOCK_GROUP) for h in range(2)]
            return attend(units, carry, None, lambda h: qaug_ref[h])

        finish(*lax.fori_loop(0, i // BLOCK_GROUP, group, (lsum_ref[0], acc_ref[0], lsum_ref[1], acc_ref[1])))

    @pl.when(jnp.logical_not(bounded))
    def _():
        qms, q_augs = prologue()
        init = []
        for head in range(2):
            s = jnp.dot(k_own(), qms[head], preferred_element_type=F32)
            s = jnp.where(key <= qry, s, -jnp.inf)
            m = jnp.max(s, axis=0, keepdims=True)
            p = jnp.exp(s - m)
            init += [m, jnp.sum(p, axis=0, keepdims=True), pv(i, head, p)]

        def body(j, carry):
            out = []
            for head in range(2):
                m, l, acc = carry[3 * head:3 * head + 3]
                s = jnp.dot(kaug_ref[j], q_augs[head], preferred_element_type=F32)
                m_new = jnp.maximum(m, jnp.max(s, axis=0, keepdims=True))
                alpha = jnp.exp(m - m_new)
                p = jnp.exp(s - m_new)
                out += [m_new, alpha * l + jnp.sum(p, axis=0, keepdims=True), alpha * acc + pv(j, head, p)]
            return tuple(out)

        res = lax.fori_loop(0, i, body, tuple(init))
        finish(res[1], res[2], res[4], res[5])


def _moba(q, k, v):
    bsz, seq, _ = q.shape
    blk = MOBA_BLOCK
    nb = seq // blk
    nbp = -(-nb // 16) * 16
    assert nbp <= LANES
    return pl.pallas_call(
        _moba_kernel,
        grid=(bsz, D_ATTN // PAIR, nb),
        in_specs=[pl.BlockSpec((1, seq, PAIR), lambda b, hp, i: (b, 0, hp))] * 3,
        out_specs=pl.BlockSpec((1, blk, PAIR), lambda b, hp, i: (b, i, hp)),
        out_shape=jax.ShapeDtypeStruct((bsz, seq, D_ATTN), BF16),
        scratch_shapes=[pltpu.VMEM((nb, blk, 2 * LANES), BF16), pltpu.VMEM((nb, PAIR, blk), BF16),
                        pltpu.VMEM((nbp, LANES), F32), pltpu.VMEM((2, 2 * LANES, blk), BF16),
                        pltpu.VMEM((2, SUBLANES, blk), F32), pltpu.VMEM((2, HEAD_DIM, blk), F32),
                        pltpu.SMEM((1,), jnp.int32)],
        compiler_params=_params("arbitrary", "arbitrary", "arbitrary"),
        name="moba",
    )(q, k, v)


def _s5_kernel(u_ref, wb_ref, are_ref, aim_ref, wc_ref, d_ref, wg_ref, bg_ref, gn_ref,
               o_ref, xh_ref, st_ref, tb_ref):
    n = SSM_GROUPS * SSM_STATE
    bsz, steps, _ = u_ref.shape
    halves = D_SSM // LANES

    @pl.when(pl.program_id(0) == 0)
    def _():
        st_ref[...] = jnp.zeros_like(st_ref)

    for b in range(bsz):
        for c in range(halves):
            tb_ref[c, pl.ds(b, steps, stride=bsz), :] = u_ref[b, :, c * LANES:(c + 1) * LANES]
    u = jnp.concatenate([tb_ref[c] for c in range(halves)], axis=1)
    ub = u.astype(BF16)
    half = u.shape[0] // 2
    row_halves = (slice(0, half), slice(half, 2 * half))
    for rs in row_halves:
        xh_ref[rs, :] = jnp.dot(ub[rs], wb_ref[...], preferred_element_type=F32)
    ar, ai = are_ref[...], aim_ref[...]

    def step(t, carry):
        hr, hi = carry
        r0 = pl.multiple_of(t * bsz, bsz)
        xr = xh_ref[pl.ds(r0, bsz), 0:n]
        xi = xh_ref[pl.ds(r0, bsz), n:2 * n]
        nr = ar * hr - ai * hi + xr
        ni = ar * hi + ai * hr + xi
        xh_ref[pl.ds(r0, bsz), 0:n] = nr
        xh_ref[pl.ds(r0, bsz), n:2 * n] = ni
        return nr, ni

    hr, hi = lax.fori_loop(0, steps, step, (st_ref[:, 0:n], st_ref[:, n:2 * n]))
    st_ref[:, 0:n] = hr
    st_ref[:, n:2 * n] = hi

    ys = [jnp.dot(xh_ref[rs, :].astype(BF16), wc_ref[...], preferred_element_type=F32) for rs in row_halves]
    zs = [_gelu_tanh(y + d_ref[...] * u[rs]) for y, rs in zip(ys, row_halves)]
    gls = [jnp.dot(z.astype(BF16), wg_ref[...], preferred_element_type=F32) + bg_ref[...] for z in zs]
    for z, gl, rs in zip(zs, gls, row_halves):
        out = _rms(z * _sigmoid(gl), gn_ref[...])
        for c in range(halves):
            tb_ref[c, rs, :] = out[:, c * LANES:(c + 1) * LANES]
    for b in range(bsz):
        for c in range(halves):
            o_ref[b, :, c * LANES:(c + 1) * LANES] = tb_ref[c, pl.ds(b, steps, stride=bsz), :].astype(o_ref.dtype)


def _s5(u, wb, a_re, a_im, wc, dsk, wg, bg, gn):
    bsz, seq, _ = u.shape
    rows = SSM_CHUNK * bsz
    n2 = 2 * SSM_GROUPS * SSM_STATE
    full = lambda r, c: pl.BlockSpec((r, c), lambda t: (0, 0))
    return pl.pallas_call(
        _s5_kernel,
        grid=(seq // SSM_CHUNK,),
        in_specs=[pl.BlockSpec((bsz, SSM_CHUNK, D_SSM), lambda t: (0, t, 0)),
                  full(D_SSM, n2), full(bsz, n2 // 2), full(bsz, n2 // 2), full(n2, D_SSM),
                  full(1, D_SSM), full(D_SSM, D_SSM), full(1, D_SSM), full(1, D_SSM)],
        out_specs=pl.BlockSpec((bsz, SSM_CHUNK, D_SSM), lambda t: (0, t, 0)),
        out_shape=jax.ShapeDtypeStruct((bsz, seq, D_SSM), BF16),
        scratch_shapes=[pltpu.VMEM((rows, n2), F32), pltpu.VMEM((bsz, n2), F32),
                        pltpu.VMEM((D_SSM // LANES, rows, LANES), F32)],
        compiler_params=_params("arbitrary"),
        name="s5",
    )(u, wb, a_re, a_im, wc, dsk, wg, bg, gn)


def _s5_matrices(lam_re, lam_im, log_dt, b_re, b_im, c_re, c_im, bsz):
    dt = jnp.exp(log_dt)[:, None]
    mag = jnp.exp(lam_re * dt)
    ab_re = mag * jnp.cos(lam_im * dt)
    ab_im = mag * jnp.sin(lam_im * dt)
    nr = ab_re - 1.0
    den = lam_re * lam_re + lam_im * lam_im
    f_re = (nr * lam_re + ab_im * lam_im) / den
    f_im = (ab_im * lam_re - nr * lam_im) / den
    bb_re = f_re[..., None] * b_re - f_im[..., None] * b_im
    bb_im = f_re[..., None] * b_im + f_im[..., None] * b_re
    eye = jnp.eye(SSM_GROUPS, dtype=F32)
    n = SSM_GROUPS * SSM_STATE
    wb = lambda bb: jnp.einsum('gpc,gh->gchp', bb, eye).reshape(D_SSM, n)
    wb_all = jnp.concatenate([wb(bb_re), wb(bb_im)], axis=1).astype(BF16)
    wc = lambda cc: jnp.einsum('gcp,gh->gphc', cc, eye).reshape(n, D_SSM)
    wc_all = jnp.concatenate([wc(c_re), -wc(c_im)], axis=0).astype(BF16)
    a_re = jnp.broadcast_to(ab_re.reshape(1, n), (bsz, n))
    a_im = jnp.broadcast_to(ab_im.reshape(1, n), (bsz, n))
    return wb_all, a_re, a_im, wc_all


def _mix_ffn_kernel(oa_ref, os_ref, oc_ref, h_ref, ga_ref, wo_ref, gf_ref, wgu_ref, wd_ref, gfin_ref,
                    o_ref, act_ref, *, d_ff, ff_chunk, final):
    an = _rms(oa_ref[0].astype(F32), ga_ref[...]).astype(BF16)
    mixed = jnp.concatenate([an, os_ref[0], oc_ref[0]], axis=1)
    h1 = h_ref[0] + jnp.dot(mixed, wo_ref[...], preferred_element_type=F32)
    hn = _rms(h1, gf_ref[...]).astype(BF16)
    for c in range(0, d_ff, ff_chunk):
        gate = jnp.dot(hn, wgu_ref[:, c:c + ff_chunk], preferred_element_type=F32)
        up = jnp.dot(hn, wgu_ref[:, d_ff + c:d_ff + c + ff_chunk], preferred_element_type=F32)
        act_ref[:, c:c + ff_chunk] = (gate * _sigmoid(gate) * up).astype(BF16)
    h2 = h1 + jnp.dot(act_ref[...], wd_ref[...], preferred_element_type=F32)
    if final:
        h2 = _rms(h2, gfin_ref[...])
    o_ref[0] = h2


def _mix_ffn(oa, os2, oc, h, ga, wo, gf, wgu, wd, gfin, layer, final):
    bsz, seq, d = h.shape
    tm = FFN_TILE
    d_ff = wd.shape[1]
    tok = lambda n: pl.BlockSpec((1, tm, n), lambda i, b: (b, i, 0))
    full = lambda r, c: pl.BlockSpec((r, c), lambda i, b: (0, 0), pipeline_mode=pl.Buffered(1))
    stack = lambda r, c: pl.BlockSpec((None, r, c), lambda i, b: (layer, 0, 0), pipeline_mode=pl.Buffered(1))
    return pl.pallas_call(
        functools.partial(_mix_ffn_kernel, d_ff=d_ff, ff_chunk=256, final=final),
        grid=(seq // tm, bsz),
        in_specs=[tok(D_ATTN), tok(D_SSM), tok(D_CONV), tok(d),
                  full(1, D_ATTN), stack(d, d), full(1, d), stack(d, 2 * d_ff), stack(d_ff, d), full(1, d)],
        out_specs=tok(d),
        out_shape=jax.ShapeDtypeStruct((bsz, seq, d), F32),
        scratch_shapes=[pltpu.VMEM((tm, d_ff), BF16)],
        compiler_params=_params("arbitrary", "arbitrary"),
        name="mix_ffn",
    )(oa, os2, oc, h, ga, wo, gf, wgu, wd, gfin)


def _rope_tables(seq):
    half = ROT_DIM // 2
    inv = jnp.power(ROPE_THETA, -jnp.arange(0, ROT_DIM, 2, dtype=F32) / ROT_DIM)
    ang = jnp.arange(seq, dtype=F32)[:, None] * inv[None, :]
    cos, sin = jnp.cos(ang), jnp.sin(ang)
    ones = jnp.ones((seq, HEAD_DIM - ROT_DIM), F32)
    zeros = jnp.zeros((seq, HEAD_DIM - ROT_DIM), F32)
    zh = jnp.zeros((seq, half), F32)
    reps = LANES // HEAD_DIM
    rc = jnp.tile(jnp.concatenate([cos, cos, ones], axis=1), (1, reps))
    rs1 = jnp.tile(jnp.concatenate([-sin, zh, zeros], axis=1), (1, reps))
    rs2 = jnp.tile(jnp.concatenate([zh, sin, zeros], axis=1), (1, reps))
    return rc, rs1, rs2


def kernel(x, norm_mix_g, w_in, lambda_re, lambda_im, log_dt, b_re, b_im, c_re, c_im, d_skip, w_ssm_glu, b_ssm_glu, conv_w, conv_b, conv_ln_g, conv_ln_b, w_conv_pw2, b_conv_pw2, g_attn_out, g_ssm_out, g_conv_out, w_out, norm_ffn_g, w_gate_up, w_down, final_norm_g):
    bsz, seq, d = x.shape
    depth = w_in.shape[0]
    rc, rs1, rs2 = _rope_tables(seq)
    row = lambda a: a.reshape(1, -1)
    w_in, w_out, w_gate_up, w_down = (w.astype(BF16) for w in (w_in, w_out, w_gate_up, w_down))
    h = x
    for l in range(depth):
        cw = jnp.pad(conv_w[l], ((0, CONV_HALO - CONV_WIDTH), (0, 0)))
        q, k, v, u_ssm, o_conv = _inproj(h, row(norm_mix_g[l]), w_in, l, rc, rs1, rs2,
                                         cw, row(conv_b[l]), row(conv_ln_g[l]), row(conv_ln_b[l]),
                                         w_conv_pw2[l].astype(BF16), row(b_conv_pw2[l]), row(g_conv_out[l]))
        o_attn = _moba(q, k, v)
        wb, a_re, a_im, wc = _s5_matrices(lambda_re[l], lambda_im[l], log_dt[l], b_re[l], b_im[l],
                                          c_re[l], c_im[l], bsz)
        o_ssm = _s5(u_ssm, wb, a_re, a_im, wc, row(d_skip[l]),
                    w_ssm_glu[l].astype(BF16), row(b_ssm_glu[l]), row(g_ssm_out[l]))
        h = _mix_ffn(o_attn, o_ssm, o_conv, h, row(g_attn_out[l]),
                     w_out, row(norm_ffn_g[l]), w_gate_up, w_down, row(final_norm_g),
                     layer=l, final=(l == depth - 1))
    return h
```

```python
import functools
import math

import jax
import jax.numpy as jnp
from jax import lax
from jax.experimental import pallas as pl
from jax.experimental.pallas import tpu as pltpu

F32 = jnp.float32
BF16 = jnp.bfloat16

ATTN_HEADS = 8
HEAD_DIM = 64
D_ATTN = ATTN_HEADS * HEAD_DIM
SSM_GROUP = 16
SSM_GROUPS = 16
SSM_STATE = 64
D_SSM = SSM_GROUP * SSM_GROUPS
D_CONV = 256
CONV_WIDTH = 31
ROT_DIM = HEAD_DIM // 4
ROPE_THETA = 500000.0
MOBA_BLOCK = 256
MOBA_TOPK = 3
RMS_EPS = 1e-6
LN_EPS = 1e-5

LANES = 128
SUBLANES = 8
PAIR = 2 * HEAD_DIM
MASK_NEG = -1e30
SCORE_BOUND = 40.0
NORM_SLACK = 1.02
BLOCK_GROUP = 16
VMEM_LIMIT = 56 * 1024 * 1024

TOK_TILE = 512
FFN_TILE = 1024
SSM_CHUNK = 128
CONV_ROWS = 64
CONV_HALO = 32


def _rms(x, g):
    return x * lax.rsqrt(jnp.mean(x * x, axis=-1, keepdims=True) + RMS_EPS) * g


def _sigmoid(x):
    return 1.0 / (1.0 + jnp.exp(-x))


def _gelu_tanh(x):
    c = math.sqrt(2.0 / math.pi)
    return 0.5 * x * (1.0 + jnp.tanh(c * (x + 0.044715 * (x * x * x))))


def _params(*sem):
    return pltpu.CompilerParams(dimension_semantics=sem, vmem_limit_bytes=VMEM_LIMIT)


def _inproj_kernel(h_ref, g_ref, w_ref, rc_ref, rs1_ref, rs2_ref,
                   cw_ref, cb_ref, lg_ref, lb_ref, pw_ref, pb_ref, gn_ref,
                   q_ref, k_ref, v_ref, us_ref, oc_ref, xx_ref, y_ref):
    tm = h_ref.shape[1]
    hn = _rms(h_ref[0], g_ref[...]).astype(BF16)

    def proj(lo, hi):
        return jnp.dot(hn, w_ref[:, lo:hi], preferred_element_type=F32)

    first = pl.program_id(1) == 0

    @pl.when(first)
    def _():
        xx_ref[0:CONV_HALO, :] = jnp.zeros((CONV_HALO, D_CONV), F32)

    @pl.when(jnp.logical_not(first))
    def _():
        xx_ref[0:CONV_HALO, :] = xx_ref[tm:tm + CONV_HALO, :]

    o = 3 * D_ATTN
    a = proj(o + D_SSM, o + D_SSM + D_CONV)
    g = proj(o + D_SSM + D_CONV, o + D_SSM + 2 * D_CONV)
    xx_ref[CONV_HALO:, :] = a * _sigmoid(g)
    rc, rs1, rs2 = rc_ref[...], rs1_ref[...], rs2_ref[...]

    def rope(t):
        return t * rc + pltpu.roll(t, LANES - ROT_DIM // 2, 1) * rs1 + pltpu.roll(t, ROT_DIM // 2, 1) * rs2

    q = proj(0, D_ATTN)
    k = proj(D_ATTN, 2 * D_ATTN)
    for c in range(D_ATTN // LANES):
        lo = c * LANES
        q_ref[0, :, lo:lo + LANES] = (rope(q[:, lo:lo + LANES]) * (HEAD_DIM ** -0.5)).astype(BF16)
        k_ref[0, :, lo:lo + LANES] = rope(k[:, lo:lo + LANES]).astype(BF16)
    v_ref[0] = proj(2 * D_ATTN, 3 * D_ATTN).astype(BF16)
    us_ref[0] = proj(o, o + D_SSM)

    base = CONV_HALO - (CONV_WIDTH - 1)
    for r in range(0, tm, CONV_ROWS):
        acc = jnp.broadcast_to(cb_ref[...], (CONV_ROWS, D_CONV))
        for s in range(SUBLANES):
            rows = CONV_ROWS + (SUBLANES if s else 0)
            z = None
            for off in range(s, CONV_HALO + 1, SUBLANES):
                j = off - base
                if 0 <= j < CONV_WIDTH:
                    term = cw_ref[j:j + 1, :] * xx_ref[r + off - s:r + off - s + rows, :]
                    z = term if z is None else z + term
            acc = acc + z[s:s + CONV_ROWS]
        y_ref[r:r + CONV_ROWS, :] = acc

    y = y_ref[...]
    mu = jnp.mean(y, axis=-1, keepdims=True)
    yc = y - mu
    yn = yc * lax.rsqrt(jnp.mean(yc * yc, axis=-1, keepdims=True) + LN_EPS) * lg_ref[...] + lb_ref[...]
    act = yn * _sigmoid(yn)
    out = jnp.dot(act.astype(BF16), pw_ref[...], preferred_element_type=F32) + pb_ref[...]
    oc_ref[0] = _rms(out, gn_ref[...]).astype(oc_ref.dtype)


def _inproj(h, g, w, layer, rc, rs1, rs2, cw, cb, lg, lb, pw, pb, gn):
    bsz, seq, d = h.shape
    tm = TOK_TILE
    d_in = w.shape[2]
    tok = lambda n: pl.BlockSpec((1, tm, n), lambda b, i: (b, i, 0))
    full = lambda r, c: pl.BlockSpec((r, c), lambda b, i: (0, 0))
    tab = pl.BlockSpec((tm, LANES), lambda b, i: (i, 0))
    return pl.pallas_call(
        _inproj_kernel,
        grid=(bsz, seq // tm),
        in_specs=[tok(d), full(1, d), pl.BlockSpec((None, d, d_in), lambda b, i: (layer, 0, 0)), tab, tab, tab,
                  full(CONV_HALO, D_CONV), full(1, D_CONV), full(1, D_CONV), full(1, D_CONV),
                  full(D_CONV, D_CONV), full(1, D_CONV), full(1, D_CONV)],
        out_specs=[tok(D_ATTN), tok(D_ATTN), tok(D_ATTN), tok(D_SSM), tok(D_CONV)],
        out_shape=[jax.ShapeDtypeStruct((bsz, seq, D_ATTN), BF16)] * 3
        + [jax.ShapeDtypeStruct((bsz, seq, D_SSM), F32),
           jax.ShapeDtypeStruct((bsz, seq, D_CONV), BF16)],
        scratch_shapes=[pltpu.VMEM((tm + CONV_HALO, D_CONV), F32), pltpu.VMEM((tm, D_CONV), F32)],
        compiler_params=_params("arbitrary", "arbitrary"),
        name="inproj",
    )(h, g, w, rc, rs1, rs2, cw, cb, lg, lb, pw, pb, gn)


def _moba_kernel(q_ref, k_ref, v_ref, o_ref, kaug_ref, vt_ref, kmean_ref, qaug_ref, lsum_ref, acc_ref,
                 flag_ref):
    i = pl.program_id(2)
    blk = MOBA_BLOCK
    nb = kaug_ref.shape[0]
    nbp = kmean_ref.shape[0]

    @pl.when(i == 0)
    def _():
        lane = lax.broadcasted_iota(jnp.int32, (blk, LANES), 1)
        kmean_ref[...] = jnp.zeros_like(kmean_ref)
        drow = lax.broadcasted_iota(jnp.int32, (LANES, LANES), 0)
        dcol = lax.broadcasted_iota(jnp.int32, (LANES, LANES), 1)
        head_sum = jnp.where(drow // HEAD_DIM == dcol, 1.0, 0.0).astype(BF16)

        def max_norm2(x, best):
            rown = jnp.dot((x * x).astype(BF16), head_sum, preferred_element_type=F32)
            return jnp.maximum(best, jnp.max(rown, axis=0, keepdims=True))

        kn2 = jnp.zeros((1, LANES), F32)
        qn2 = jnp.zeros((1, LANES), F32)
        for j in range(nb):
            kj = k_ref[0, j * blk:(j + 1) * blk, :]
            kf = kj.astype(F32)
            kaug_ref[j, :, 0:LANES] = kj
            kaug_ref[j, :, LANES:2 * LANES] = jnp.where(lane == j, 1.0, 0.0).astype(BF16)
            kmean_ref[j:j + 1, :] = jnp.mean(kf, axis=0, keepdims=True)
            vt_ref[j] = v_ref[0, j * blk:(j + 1) * blk, :].astype(F32).T.astype(BF16)
            kn2 = max_norm2(kf, kn2)
            qn2 = max_norm2(q_ref[0, j * blk:(j + 1) * blk, :].astype(F32), qn2)
        bound2 = jnp.max(qn2 * kn2) * (NORM_SLACK * NORM_SLACK)
        flag_ref[0] = (bound2 < SCORE_BOUND * SCORE_BOUND).astype(jnp.int32)

    bounded = flag_ref[0] == 1
    off_i = pl.multiple_of(i * blk, blk)
    key = lax.broadcasted_iota(jnp.int32, (blk, blk), 0)
    qry = lax.broadcasted_iota(jnp.int32, (blk, blk), 1)

    def prologue():
        qt = q_ref[0, pl.ds(off_i, blk), :].astype(F32).T.astype(BF16)
        dim = lax.broadcasted_iota(jnp.int32, (PAIR, blk), 0)
        cand = lax.broadcasted_iota(jnp.int32, (nbp, blk), 0)
        kmean = kmean_ref[...]
        km_hi = kmean.astype(BF16)
        km_lo = (kmean - km_hi.astype(F32)).astype(BF16)
        pad = jnp.zeros((LANES - nbp, blk), BF16)
        qms, q_augs = [], []
        for head in range(2):
            in_head = (dim < HEAD_DIM) if head == 0 else (dim >= HEAD_DIM)
            qm = jnp.where(in_head, qt, jnp.zeros_like(qt))
            qms.append(qm)
            gate = (jnp.dot(km_hi, qm, preferred_element_type=F32)
                    + jnp.dot(km_lo, qm, preferred_element_type=F32))
            valid = cand < i
            gm = jnp.where(valid, gate, -jnp.inf)
            ahead = jnp.zeros((nbp, blk), jnp.int32)
            for n in range(nb - 1):
                gn = gm[n:n + 1, :]
                beats = (gn > gm) | ((gn == gm) & (cand > n))
                ahead = ahead + jnp.where(beats, 1, 0)
            sel = valid & (ahead < MOBA_TOPK)
            pen = jnp.where(sel, 0.0, MASK_NEG).astype(BF16)
            q_augs.append(jnp.concatenate([qm, pen, pad], axis=0))
        return qms, q_augs

    def k_own():
        return k_ref[0, pl.ds(off_i, blk), :]

    def finish(l_a, acc_a, l_b, acc_b):
        inv_a = 1.0 / jnp.sum(l_a, axis=0, keepdims=True)
        inv_b = 1.0 / jnp.sum(l_b, axis=0, keepdims=True)
        o = jnp.concatenate([acc_a * inv_a, acc_b * inv_b], axis=0)
        o_ref[0] = o.T.astype(o_ref.dtype)

    def pv(j, h, p):
        return jnp.dot(vt_ref[j, h * HEAD_DIM:(h + 1) * HEAD_DIM, :], p.astype(BF16),
                       preferred_element_type=F32)

    def colsum8(p):
        return jnp.sum(p.reshape(p.shape[0] // SUBLANES, SUBLANES, p.shape[1]), axis=0)

    @pl.when(bounded)
    def _():
        def attend(units, carry, qm, q_aug):
            scores = [jnp.dot(k_own(), qm(h), preferred_element_type=F32) if j is None
                      else jnp.dot(kaug_ref[j], q_aug(h), preferred_element_type=F32) for j, h in units]
            probs = [jnp.where(key <= qry, jnp.exp(s), 0.0) if j is None else jnp.exp(s)
                     for (j, h), s in zip(units, scores)]
            out = list(carry)
            for (j, h), p in zip(units, probs):
                out[2 * h] = colsum8(p) if out[2 * h] is None else out[2 * h] + colsum8(p)
            for (j, h), p in zip(units, probs):
                o = pv(i if j is None else j, h, p)
                out[2 * h + 1] = o if out[2 * h + 1] is None else out[2 * h + 1] + o
            return tuple(out)

        rem = i % BLOCK_GROUP
        looped = nb > BLOCK_GROUP
        for r in range(min(BLOCK_GROUP, nb)):
            @pl.when(rem == r)
            def _():
                qms, q_augs = prologue()
                units = [(None, 0), (None, 1)] + [(g, h) for g in range(r) for h in range(2)]
                res = attend(units, (None,) * 4, lambda h: qms[h], lambda h: q_augs[h])
                if not looped:
                    finish(*res)
                    return
                for h in range(2):
                    qaug_ref[h] = q_augs[h]
                    lsum_ref[h] = res[2 * h]
                    acc_ref[h] = res[2 * h + 1]

        if looped:
            def group(g, carry):
                j0 = rem + g * BLOCK_GROUP
                units = [(j0 + t, h) for t in range(BLOCK_GROUP) for h in range(2)]
                return attend(units, carry, None, lambda h: qaug_ref[h])

            finish(*lax.fori_loop(0, i // BLOCK_GROUP, group,
                                  (lsum_ref[0], acc_ref[0], lsum_ref[1], acc_ref[1])))

    @pl.when(jnp.logical_not(bounded))
    def _():
        qms, q_augs = prologue()
        init = []
        for head in range(2):
            s = jnp.dot(k_own(), qms[head], preferred_element_type=F32)
            s = jnp.where(key <= qry, s, -jnp.inf)
            m = jnp.max(s, axis=0, keepdims=True)
            p = jnp.exp(s - m)
            init += [m, jnp.sum(p, axis=0, keepdims=True), pv(i, head, p)]

        def body(j, carry):
            out = []
            for head in range(2):
                m, l, acc = carry[3 * head:3 * head + 3]
                s = jnp.dot(kaug_ref[j], q_augs[head], preferred_element_type=F32)
                m_new = jnp.maximum(m, jnp.max(s, axis=0, keepdims=True))
                alpha = jnp.exp(m - m_new)
                p = jnp.exp(s - m_new)
                out += [m_new, alpha * l + jnp.sum(p, axis=0, keepdims=True), alpha * acc + pv(j, head, p)]
            return tuple(out)

        res = lax.fori_loop(0, i, body, tuple(init))
        finish(res[1], res[2], res[4], res[5])


def _moba(q, k, v):
    bsz, seq, _ = q.shape
    blk = MOBA_BLOCK
    nb = seq // blk
    nbp = -(-nb // 16) * 16
    assert nbp <= LANES
    return pl.pallas_call(
        _moba_kernel,
        grid=(bsz, D_ATTN // PAIR, nb),
        in_specs=[pl.BlockSpec((1, seq, PAIR), lambda b, hp, i: (b, 0, hp))] * 3,
        out_specs=pl.BlockSpec((1, blk, PAIR), lambda b, hp, i: (b, i, hp)),
        out_shape=jax.ShapeDtypeStruct((bsz, seq, D_ATTN), BF16),
        scratch_shapes=[pltpu.VMEM((nb, blk, 2 * LANES), BF16), pltpu.VMEM((nb, PAIR, blk), BF16),
                        pltpu.VMEM((nbp, LANES), F32), pltpu.VMEM((2, 2 * LANES, blk), BF16),
                        pltpu.VMEM((2, SUBLANES, blk), F32), pltpu.VMEM((2, HEAD_DIM, blk), F32),
                        pltpu.SMEM((1,), jnp.int32)],
        compiler_params=_params("arbitrary", "arbitrary", "arbitrary"),
        name="moba",
    )(q, k, v)


def _s5_kernel(u_ref, wb_ref, are_ref, aim_ref, wc_ref, d_ref, wg_ref, bg_ref, gn_ref,
               o_ref, xh_ref, st_ref, tb_ref):
    n = SSM_GROUPS * SSM_STATE
    bsz, steps, _ = u_ref.shape
    halves = D_SSM // LANES

    @pl.when(pl.program_id(0) == 0)
    def _():
        st_ref[...] = jnp.zeros_like(st_ref)

    for b in range(bsz):
        for c in range(halves):
            tb_ref[c, pl.ds(b, steps, stride=bsz), :] = u_ref[b, :, c * LANES:(c + 1) * LANES]
    u = jnp.concatenate([tb_ref[c] for c in range(halves)], axis=1)
    ub = u.astype(BF16)
    half = u.shape[0] // 2
    row_halves = (slice(0, half), slice(half, 2 * half))
    for rs in row_halves:
        xh_ref[rs, :] = jnp.dot(ub[rs], wb_ref[...], preferred_element_type=F32)
    ar, ai = are_ref[...], aim_ref[...]

    def step(t, carry):
        hr, hi = carry
        r0 = pl.multiple_of(t * bsz, bsz)
        xr = xh_ref[pl.ds(r0, bsz), 0:n]
        xi = xh_ref[pl.ds(r0, bsz), n:2 * n]
        nr = ar * hr - ai * hi + xr
        ni = ar * hi + ai * hr + xi
        xh_ref[pl.ds(r0, bsz), 0:n] = nr
        xh_ref[pl.ds(r0, bsz), n:2 * n] = ni
        return nr, ni

    hr, hi = lax.fori_loop(0, steps, step, (st_ref[:, 0:n], st_ref[:, n:2 * n]))
    st_ref[:, 0:n] = hr
    st_ref[:, n:2 * n] = hi

    ys = [jnp.dot(xh_ref[rs, :].astype(BF16), wc_ref[...], preferred_element_type=F32) for rs in row_halves]
    zs = [_gelu_tanh(y + d_ref[...] * u[rs]) for y, rs in zip(ys, row_halves)]
    gls = [jnp.dot(z.astype(BF16), wg_ref[...], preferred_element_type=F32) + bg_ref[...] for z in zs]
    for z, gl, rs in zip(zs, gls, row_halves):
        out = _rms(z * _sigmoid(gl), gn_ref[...])
        for c in range(halves):
            tb_ref[c, rs, :] = out[:, c * LANES:(c + 1) * LANES]
    for b in range(bsz):
        for c in range(halves):
            o_ref[b, :, c * LANES:(c + 1) * LANES] = tb_ref[c, pl.ds(b, steps, stride=bsz), :].astype(o_ref.dtype)


def _s5(u, wb, a_re, a_im, wc, dsk, wg, bg, gn):
    bsz, seq, _ = u.shape
    rows = SSM_CHUNK * bsz
    n2 = 2 * SSM_GROUPS * SSM_STATE
    full = lambda r, c: pl.BlockSpec((r, c), lambda t: (0, 0))
    return pl.pallas_call(
        _s5_kernel,
        grid=(seq // SSM_CHUNK,),
        in_specs=[pl.BlockSpec((bsz, SSM_CHUNK, D_SSM), lambda t: (0, t, 0)),
                  full(D_SSM, n2), full(bsz, n2 // 2), full(bsz, n2 // 2), full(n2, D_SSM),
                  full(1, D_SSM), full(D_SSM, D_SSM), full(1, D_SSM), full(1, D_SSM)],
        out_specs=pl.BlockSpec((bsz, SSM_CHUNK, D_SSM), lambda t: (0, t, 0)),
        out_shape=jax.ShapeDtypeStruct((bsz, seq, D_SSM), BF16),
        scratch_shapes=[pltpu.VMEM((rows, n2), F32), pltpu.VMEM((bsz, n2), F32),
                        pltpu.VMEM((D_SSM // LANES, rows, LANES), F32)],
        compiler_params=_params("arbitrary"),
        name="s5",
    )(u, wb, a_re, a_im, wc, dsk, wg, bg, gn)


def _s5_matrices(lam_re, lam_im, log_dt, b_re, b_im, c_re, c_im, bsz):
    dt = jnp.exp(log_dt)[:, None]
    mag = jnp.exp(lam_re * dt)
    ab_re = mag * jnp.cos(lam_im * dt)
    ab_im = mag * jnp.sin(lam_im * dt)
    nr = ab_re - 1.0
    den = lam_re * lam_re + lam_im * lam_im
    f_re = (nr * lam_re + ab_im * lam_im) / den
    f_im = (ab_im * lam_re - nr * lam_im) / den
    bb_re = f_re[..., None] * b_re - f_im[..., None] * b_im
    bb_im = f_re[..., None] * b_im + f_im[..., None] * b_re
    eye = jnp.eye(SSM_GROUPS, dtype=F32)
    n = SSM_GROUPS * SSM_STATE
    wb = lambda bb: jnp.einsum('gpc,gh->gchp', bb, eye).reshape(D_SSM, n)
    wb_all = jnp.concatenate([wb(bb_re), wb(bb_im)], axis=1).astype(BF16)
    wc = lambda cc: jnp.einsum('gcp,gh->gphc', cc, eye).reshape(n, D_SSM)
    wc_all = jnp.concatenate([wc(c_re), -wc(c_im)], axis=0).astype(BF16)
    a_re = jnp.broadcast_to(ab_re.reshape(1, n), (bsz, n))
    a_im = jnp.broadcast_to(ab_im.reshape(1, n), (bsz, n))
    return wb_all, a_re, a_im, wc_all


def _mix_ffn_kernel(oa_ref, os_ref, oc_ref, h_ref, ga_ref, wo_ref, gf_ref, wgu_ref, wd_ref, gfin_ref,
                    o_ref, act_ref, *, d_ff, ff_chunk, final):
    an = _rms(oa_ref[0].astype(F32), ga_ref[...]).astype(BF16)
    mixed = jnp.concatenate([an, os_ref[0], oc_ref[0]], axis=1)
    h1 = h_ref[0] + jnp.dot(mixed, wo_ref[...], preferred_element_type=F32)
    hn = _rms(h1, gf_ref[...]).astype(BF16)
    for c in range(0, d_ff, ff_chunk):
        gate = jnp.dot(hn, wgu_ref[:, c:c + ff_chunk], preferred_element_type=F32)
        up = jnp.dot(hn, wgu_ref[:, d_ff + c:d_ff + c + ff_chunk], preferred_element_type=F32)
        act_ref[:, c:c + ff_chunk] = (gate * _sigmoid(gate) * up).astype(BF16)
    h2 = h1 + jnp.dot(act_ref[...], wd_ref[...], preferred_element_type=F32)
    if final:
        h2 = _rms(h2, gfin_ref[...])
    o_ref[0] = h2


def _mix_ffn(oa, os2, oc, h, ga, wo, gf, wgu, wd, gfin, layer, final):
    bsz, seq, d = h.shape
    tm = FFN_TILE
    d_ff = wd.shape[1]
    tok = lambda n: pl.BlockSpec((1, tm, n), lambda i, b: (b, i, 0))
    full = lambda r, c: pl.BlockSpec((r, c), lambda i, b: (0, 0), pipeline_mode=pl.Buffered(1))
    stack = lambda r, c: pl.BlockSpec((None, r, c), lambda i, b: (layer, 0, 0), pipeline_mode=pl.Buffered(1))
    return pl.pallas_call(
        functools.partial(_mix_ffn_kernel, d_ff=d_ff, ff_chunk=256, final=final),
        grid=(seq // tm, bsz),
        in_specs=[tok(D_ATTN), tok(D_SSM), tok(D_CONV), tok(d),
                  full(1, D_ATTN), stack(d, d), full(1, d), stack(d, 2 * d_ff), stack(d_ff, d), full(1, d)],
        out_specs=tok(d),
        out_shape=jax.ShapeDtypeStruct((bsz, seq, d), F32),
        scratch_shapes=[pltpu.VMEM((tm, d_ff), BF16)],
        compiler_params=_params("arbitrary", "arbitrary"),
        name="mix_ffn",
    )(oa, os2, oc, h, ga, wo, gf, wgu, wd, gfin)


def _rope_tables(seq):
    half = ROT_DIM // 2
    inv = jnp.power(ROPE_THETA, -jnp.arange(0, ROT_DIM, 2, dtype=F32) / ROT_DIM)
    ang = jnp.arange(seq, dtype=F32)[:, None] * inv[None, :]
    cos, sin = jnp.cos(ang), jnp.sin(ang)
    ones = jnp.ones((seq, HEAD_DIM - ROT_DIM), F32)
    zeros = jnp.zeros((seq, HEAD_DIM - ROT_DIM), F32)
    zh = jnp.zeros((seq, half), F32)
    reps = LANES // HEAD_DIM
    rc = jnp.tile(jnp.concatenate([cos, cos, ones], axis=1), (1, reps))
    rs1 = jnp.tile(jnp.concatenate([-sin, zh, zeros], axis=1), (1, reps))
    rs2 = jnp.tile(jnp.concatenate([zh, sin, zeros], axis=1), (1, reps))
    return rc, rs1, rs2


def kernel(x, norm_mix_g, w_in, lambda_re, lambda_im, log_dt, b_re, b_im, c_re, c_im, d_skip, w_ssm_glu, b_ssm_glu, conv_w, conv_b, conv_ln_g, conv_ln_b, w_conv_pw2, b_conv_pw2, g_attn_out, g_ssm_out, g_conv_out, w_out, norm_ffn_g, w_gate_up, w_down, final_norm_g):
    bsz, seq, d = x.shape
    depth = w_in.shape[0]
    rc, rs1, rs2 = _rope_tables(seq)
    row = lambda a: a.reshape(1, -1)
    w_in, w_out, w_gate_up, w_down = (w.astype(BF16) for w in (w_in, w_out, w_gate_up, w_down))
    h = x
    for l in range(depth):
        cw = jnp.pad(conv_w[l], ((0, CONV_HALO - CONV_WIDTH), (0, 0)))
        q, k, v, u_ssm, o_conv = _inproj(h, row(norm_mix_g[l]), w_in, l, rc, rs1, rs2,
                                         cw, row(conv_b[l]), row(conv_ln_g[l]), row(conv_ln_b[l]),
                                         w_conv_pw2[l].astype(BF16), row(b_conv_pw2[l]), row(g_conv_out[l]))
        o_attn = _moba(q, k, v)
        wb, a_re, a_im, wc = _s5_matrices(lambda_re[l], lambda_im[l], log_dt[l], b_re[l], b_im[l],
                                          c_re[l], c_im[l], bsz)
        o_ssm = _s5(u_ssm, wb, a_re, a_im, wc, row(d_skip[l]),
                    w_ssm_glu[l].astype(BF16), row(b_ssm_glu[l]), row(g_ssm_out[l]))
        h = _mix_ffn(o_attn, o_ssm, o_conv, h, row(g_attn_out[l]),
                     w_out, row(norm_ffn_g[l]), w_gate_up, w_down, row(final_norm_g),
                     layer=l, final=(l == depth - 1))
    return h
```
